```python
import jax, jax.numpy as jnp
from jax import lax
import numpy as np

D_MODEL = 2048
BATCH = 1
SEQ = 8192
DEPTH = 2
DEC_BATCH = 16
DEC_SEQ = 32
PAST_LEN = 2048

CHUNK = 64
N_EVEN = (DEPTH + 1) // 2
N_ODD = DEPTH // 2
EPS = 1e-6
HEAD_DIM = 64
A_HEADS = D_MODEL // 2 // HEAD_DIM
A_KV_HEADS = A_HEADS // 4
A_GROUP = A_HEADS // A_KV_HEADS
WINDOW = 128
WIN_CHUNKS = WINDOW // CHUNK
ROT_DIM = HEAD_DIM // 4
ROPE_THETA = 500000.0
A_WIDTH = A_HEADS * HEAD_DIM
KV_WIDTH = A_KV_HEADS * HEAD_DIM
GMLP_CHUNK = 128
GMLP_DIM = 128
B_GROUPS = D_MODEL // 2 // GMLP_DIM
B_WIDTH = B_GROUPS * GMLP_DIM
EVEN_IN = A_WIDTH + 2 * KV_WIDTH + 2 * B_WIDTH
EVEN_OUT = A_WIDTH + B_WIDTH
C_HEAD_DIM = 128
C_HEADS = D_MODEL // C_HEAD_DIM
C_WIDTH = C_HEADS * C_HEAD_DIM
ODD_IN = 4 * C_WIDTH
HGRN_BLOCK = 16
D_FF = 4 * D_MODEL

kernel_name = 'hybrid_streaming_swa_gmlp_hgrn2_step'


def rms_norm(x, g):
    xf = x.astype(jnp.float32)
    y = xf * lax.rsqrt(jnp.mean(xf * xf, axis=-1, keepdims=True) + EPS)
    return (y * g.astype(jnp.float32)).astype(x.dtype)


def layer_norm(x, g):
    xf = x.astype(jnp.float32)
    mu = jnp.mean(xf, axis=-1, keepdims=True)
    xc = xf - mu
    y = xc * lax.rsqrt(jnp.mean(xc * xc, axis=-1, keepdims=True) + EPS)
    return (y * g.astype(jnp.float32)).astype(x.dtype)


def partial_rope(x, pos):
    half = ROT_DIM // 2
    inv = ROPE_THETA ** (-(jnp.arange(half, dtype=jnp.float32) * (2.0 / ROT_DIM)))
    ang = pos.astype(jnp.float32)[:, None] * inv[None, :]
    cos = jnp.cos(ang)[None, :, None, :]
    sin = jnp.sin(ang)[None, :, None, :]
    xr = x[..., :ROT_DIM].astype(jnp.float32)
    x1, x2 = xr[..., :half], xr[..., half:]
    rot = jnp.concatenate([x1 * cos - x2 * sin, x2 * cos + x1 * sin], axis=-1).astype(x.dtype)
    return jnp.concatenate([rot, x[..., ROT_DIM:]], axis=-1)


def sink_attention(q, k, v, sinks, mask):
    s = jnp.einsum('...qkgd,...skd->...kgqs', q, k, preferred_element_type=jnp.float32) * (HEAD_DIM ** -0.5)
    if mask is not None:
        s = jnp.where(mask, s, -jnp.inf)
    sink = jnp.broadcast_to(sinks.astype(jnp.float32)[:, :, None, None], s.shape[:-1] + (1,))
    p = jax.nn.softmax(jnp.concatenate([s, sink], axis=-1), axis=-1)[..., :-1]
    return jnp.einsum('...kgqs,...skd->...qkgd', p.astype(v.dtype), v)


def swa_prompt(q, k, v, sinks):
    B, S = q.shape[:2]
    nb = S // CHUNK
    qb = q.reshape(B, nb, CHUNK, A_KV_HEADS, A_GROUP, HEAD_DIM)
    pad = ((0, 0), (WIN_CHUNKS * CHUNK, 0), (0, 0), (0, 0))

    def band(t):
        tb = jnp.pad(t, pad).reshape(B, nb + WIN_CHUNKS, CHUNK, A_KV_HEADS, HEAD_DIM)
        return jnp.concatenate([tb[:, j:j + nb] for j in range(WIN_CHUNKS + 1)], axis=2)

    blk = jnp.arange(nb)[:, None] + jnp.arange(WIN_CHUNKS + 1)[None, :] - WIN_CHUNKS
    valid = jnp.repeat(blk >= 0, CHUNK, axis=1)
    mask = valid[:, None, None, None, :]
    o = sink_attention(qb, band(k), band(v), sinks, mask)
    return o.reshape(B, S, A_WIDTH)


def chunk_gmlp(u, v, ws, bias):
    B, T = v.shape[:2]
    tp = -(-T // GMLP_CHUNK) * GMLP_CHUNK
    vp = jnp.pad(v, ((0, 0), (0, tp - T), (0, 0), (0, 0)))
    vp = vp.reshape(B, tp // GMLP_CHUNK, GMLP_CHUNK, B_GROUPS, GMLP_DIM)
    cidx = jnp.arange(GMLP_CHUNK) // CHUNK
    w = jnp.where(cidx[None, :, None] >= cidx[None, None, :], ws, 0.0).astype(v.dtype)
    mixed = jnp.einsum('hpq,bcqhd->bcphd', w, vp) + bias.T.astype(v.dtype)[None, None, :, :, None]
    mixed = mixed.reshape(B, tp, B_GROUPS, GMLP_DIM)[:, :T]
    return u * mixed


def even_layer(xn, pos, w_in, w_out, sinks, g_vnorm, ws, bias, cache_k, cache_v):
    B, T, _ = xn.shape
    cuts = [A_WIDTH, A_WIDTH + KV_WIDTH, A_WIDTH + 2 * KV_WIDTH, A_WIDTH + 2 * KV_WIDTH + B_WIDTH]
    q, k, v, zu, zv = jnp.split(xn @ w_in, cuts, axis=-1)
    q = partial_rope(q.reshape(B, T, A_HEADS, HEAD_DIM), pos)
    k = partial_rope(k.reshape(B, T, A_KV_HEADS, HEAD_DIM), pos)
    v = v.reshape(B, T, A_KV_HEADS, HEAD_DIM)
    sk = sinks.reshape(A_KV_HEADS, A_GROUP)
    if cache_k is None:
        a = swa_prompt(q, k, v, sk)
        k_all, v_all = k, v
    else:
        k_all = jnp.concatenate([cache_k, k], axis=1)
        v_all = jnp.concatenate([cache_v, v], axis=1)
        qg = q.reshape(B, T, A_KV_HEADS, A_GROUP, HEAD_DIM)
        a = sink_attention(qg, k_all, v_all, sk, None).reshape(B, T, A_WIDTH)
    u = jax.nn.gelu(zu).reshape(B, T, B_GROUPS, GMLP_DIM)
    gv = layer_norm(jax.nn.gelu(zv).reshape(B, T, B_GROUPS, GMLP_DIM), g_vnorm.reshape(B_GROUPS, GMLP_DIM))
    b = chunk_gmlp(u, gv, ws, bias).reshape(B, T, B_WIDTH)
    y = jnp.concatenate([a, b], axis=-1) @ w_out
    return y, k_all[:, -WINDOW:], v_all[:, -WINDOW:], gv


def hgrn2_recurrence(q, k, v, logf, s0, block):
    B, T, H, _ = q.shape
    n = T // block

    def blocks(t):
        return jnp.moveaxis(t.reshape(B, n, block, H, t.shape[-1]), 1, 0)

    tri = jnp.tril(jnp.ones((block, block), dtype=bool))[None, :, :, None, None]

    def step(S, inp):
        qc, kc, vc, lfc = inp
        b = jnp.cumsum(lfc, axis=1)
        dec = jnp.exp(jnp.where(tri, b[:, :, None] - b[:, None, :], -jnp.inf))
        scores = jnp.einsum('bthd,bshd,btshd->bhts', qc, kc, dec)
        o = jnp.einsum('bhts,bshe->bthe', scores, vc) + jnp.einsum('bthd,bhde->bthe', qc * jnp.exp(b), S)
        b_last = b[:, -1]
        S = jnp.exp(b_last)[..., None] * S + jnp.einsum('bshd,bshe->bhde', kc * jnp.exp(b_last[:, None] - b), vc)
        return S, o

    S, o = lax.scan(step, s0, (blocks(q), blocks(k), blocks(v), blocks(logf)))
    return jnp.moveaxis(o, 0, 1).reshape(B, T, H, v.shape[-1]), S


def odd_layer(xn, w_in, w_out, lb, g_onorm, state0, block):
    B, T, _ = xn.shape
    zq, zf, zi, zg = jnp.split(xn @ w_in, 4, axis=-1)
    shp = (B, T, C_HEADS, C_HEAD_DIM)
    f = lb + (1.0 - lb) * jax.nn.sigmoid(zf.astype(jnp.float32))
    q = jax.nn.silu(zq.astype(jnp.float32)).reshape(shp)
    k = (1.0 - f).reshape(shp)
    logf = jnp.log(f).reshape(shp)
    o, S = hgrn2_recurrence(q, k, zi.astype(jnp.float32).reshape(shp), logf, state0.astype(jnp.float32), block)
    o = rms_norm(o, g_onorm.reshape(C_HEADS, C_HEAD_DIM)).reshape(B, T, C_WIDTH)
    o = (o * jax.nn.sigmoid(zg.astype(jnp.float32))).astype(xn.dtype)
    return o @ w_out, S


def sq_relu_mlp(xn, w_up, w_down):
    return jnp.square(jax.nn.relu(xn @ w_up)) @ w_down


def setup_inputs(seed: int = 0) -> dict:
    key = jax.random.key(seed)
    ks = jax.random.split(key, 21)

    def nrm(k, shape, scale):
        return jax.random.normal(k, shape, jnp.float32) * scale

    return {
        'x_prompt': nrm(ks[0], (BATCH, SEQ, D_MODEL), 1.0),
        'x_sample': nrm(ks[1], (DEC_BATCH, DEC_SEQ, D_MODEL), 1.0),
        'cache_swa_k': nrm(ks[2], (N_EVEN, DEC_BATCH, WINDOW, A_KV_HEADS, HEAD_DIM), 1.0),
        'cache_swa_v': nrm(ks[3], (N_EVEN, DEC_BATCH, WINDOW, A_KV_HEADS, HEAD_DIM), 1.0),
        'state_hgrn': nrm(ks[4], (N_ODD, DEC_BATCH, C_HEADS, C_HEAD_DIM, C_HEAD_DIM), 0.5),
        'norm_mix': 1.0 + nrm(ks[5], (DEPTH, D_MODEL), 0.02),
        'norm_ffn': 1.0 + nrm(ks[6], (DEPTH, D_MODEL), 0.02),
        'norm_final': 1.0 + nrm(ks[7], (D_MODEL,), 0.02),
        'w_in_even': nrm(ks[8], (N_EVEN, D_MODEL, EVEN_IN), D_MODEL ** -0.5),
        'w_out_even': nrm(ks[9], (N_EVEN, EVEN_OUT, D_MODEL), EVEN_OUT ** -0.5),
        'attn_sinks': nrm(ks[10], (N_EVEN, A_HEADS), 1.0),
        'gmlp_vnorm': 1.0 + nrm(ks[11], (N_EVEN, B_WIDTH), 0.02),
        'gmlp_ws': nrm(ks[12], (N_EVEN, B_GROUPS, GMLP_CHUNK, GMLP_CHUNK), GMLP_CHUNK ** -0.5),
        'gmlp_bias': 1.0 + nrm(ks[13], (N_EVEN, B_GROUPS, GMLP_CHUNK), 0.1),
        'w_in_odd': nrm(ks[14], (N_ODD, D_MODEL, ODD_IN), D_MODEL ** -0.5),
        'w_out_odd': nrm(ks[15], (N_ODD, C_WIDTH, D_MODEL), C_WIDTH ** -0.5),
        'hgrn_lb': 1.0 + nrm(ks[16], (DEPTH, C_WIDTH), 0.1),
        'hgrn_onorm': 1.0 + nrm(ks[17], (N_ODD, C_WIDTH), 0.02),
        'w_ffn_up': nrm(ks[18], (DEPTH, D_MODEL, D_FF), D_MODEL ** -0.5),
        'w_ffn_down': nrm(ks[19], (DEPTH, D_FF, D_MODEL), D_FF ** -0.5),
    }


def reference(x_prompt, x_sample, cache_swa_k, cache_swa_v, state_hgrn, norm_mix, norm_ffn, norm_final,
              w_in_even, w_out_even, attn_sinks, gmlp_vnorm, gmlp_ws, gmlp_bias, w_in_odd, w_out_odd,
              hgrn_lb, hgrn_onorm, w_ffn_up, w_ffn_down):
    pos_p = jnp.arange(x_prompt.shape[1], dtype=jnp.int32)
    pos_s = PAST_LEN + jnp.arange(x_sample.shape[1], dtype=jnp.int32)
    p_lb = jax.nn.softmax(hgrn_lb.astype(jnp.float32), axis=0)
    lb_all = jnp.cumsum(p_lb, axis=0) - p_lb[0]
    hp, hs = x_prompt, x_sample
    kp, vp, ks, vs, gs, sp, ss = [], [], [], [], [], [], []
    for l in range(DEPTH):
        e = l // 2
        xp_n = rms_norm(hp, norm_mix[l])
        xs_n = rms_norm(hs, norm_mix[l])
        if l % 2 == 0:
            yp, nkp, nvp, _ = even_layer(xp_n, pos_p, w_in_even[e], w_out_even[e], attn_sinks[e], gmlp_vnorm[e],
                                         gmlp_ws[e], gmlp_bias[e], None, None)
            ys, nks, nvs, gvs = even_layer(xs_n, pos_s, w_in_even[e], w_out_even[e], attn_sinks[e], gmlp_vnorm[e],
                                           gmlp_ws[e], gmlp_bias[e], cache_swa_k[e], cache_swa_v[e])
            kp.append(nkp); vp.append(nvp); ks.append(nks); vs.append(nvs); gs.append(gvs)
        else:
            s0 = jnp.zeros((x_prompt.shape[0], C_HEADS, C_HEAD_DIM, C_HEAD_DIM), jnp.float32)
            yp, Sp = odd_layer(xp_n, w_in_odd[e], w_out_odd[e], lb_all[l], hgrn_onorm[e], s0, HGRN_BLOCK)
            ys, Ss = odd_layer(xs_n, w_in_odd[e], w_out_odd[e], lb_all[l], hgrn_onorm[e], state_hgrn[e],
                               x_sample.shape[1])
            sp.append(Sp.astype(state_hgrn.dtype)); ss.append(Ss.astype(state_hgrn.dtype))
        hp = hp + yp
        hs = hs + ys
        hp = hp + sq_relu_mlp(rms_norm(hp, norm_ffn[l]), w_ffn_up[l], w_ffn_down[l])
        hs = hs + sq_relu_mlp(rms_norm(hs, norm_ffn[l]), w_ffn_up[l], w_ffn_down[l])
    y_prompt = rms_norm(hp, norm_final)
    y_sample = rms_norm(hs, norm_final)
    return (y_prompt, y_sample, jnp.stack(kp), jnp.stack(vp), jnp.stack(ks), jnp.stack(vs), jnp.stack(gs),
            jnp.stack(sp), jnp.stack(ss))
```

```python
import functools
import math

import jax
import jax.numpy as jnp
from jax import lax
from jax.experimental import pallas as pl
from jax.experimental.pallas import tpu as pltpu

F32 = jnp.float32
BF16 = jnp.bfloat16

D_MODEL = 2048
N_PROMPT = 8192
DEC_BATCH = 16
DEC_SEQ = 32
N_SAMPLE = DEC_BATCH * DEC_SEQ
N_TOK = N_PROMPT + N_SAMPLE
PAST_LEN = 2048
EPS = 1e-6

HEAD_DIM = 64
A_HEADS = 16
A_KV_HEADS = 4
A_WIDTH = A_HEADS * HEAD_DIM
KV_WIDTH = A_KV_HEADS * HEAD_DIM
ROT_DIM = 16
ROPE_THETA = 500000.0
WINDOW = 128
CHUNK = 64
B_GROUPS = 8
GMLP_DIM = 128
B_WIDTH = B_GROUPS * GMLP_DIM
EVEN_IN = A_WIDTH + 2 * KV_WIDTH + 2 * B_WIDTH
C_HEADS = 16
C_DIM = 128
D_FF = 4 * D_MODEL

LANES = 128
TM = 512
MIX_TM = 128
HG_CHUNK = 128
FF_BLOCK = 512
VMEM_LIMIT = 56 * 1024 * 1024

_NT = (((1,), (1,)), ((), ()))


def _params(sem):
    return pltpu.CompilerParams(dimension_semantics=sem, vmem_limit_bytes=VMEM_LIMIT)


def _rms(x, g):
    return x * lax.rsqrt(jnp.mean(x * x, axis=-1, keepdims=True) + EPS) * g


def _sigmoid(x):
    return 1.0 / (1.0 + jnp.exp(-x))


def _gelu(x):
    c = math.sqrt(2.0 / math.pi)
    return x * (0.5 * (1.0 + jnp.tanh(c * (x + 0.044715 * (x * x * x)))))


def _dot(a, b):
    return jnp.dot(a, b, preferred_element_type=F32)


def _even_in_kernel(h_ref, g_ref, w_ref, ca_ref, cb_ref, gv_g_ref,
                    q_ref, k_ref, v_ref, u_ref, gv_ref):
    xn = _rms(h_ref[...], g_ref[...]).astype(BF16)
    ca = ca_ref[...]
    cb = cb_ref[...]
    lane = lax.broadcasted_iota(jnp.int32, (TM, LANES), 1)
    first = (lane & (HEAD_DIM - 1)) < (ROT_DIM // 2)

    def rope(x):
        partner = jnp.where(first, pltpu.roll(x, LANES - ROT_DIM // 2, 1), pltpu.roll(x, ROT_DIM // 2, 1))
        return x * ca + partner * cb

    zq = _dot(xn, w_ref[:, 0:A_WIDTH])
    for t in range(A_WIDTH // LANES):
        sl = slice(t * LANES, (t + 1) * LANES)
        q_ref[:, sl] = (rope(zq[:, sl]) * (HEAD_DIM ** -0.5)).astype(BF16)
    zkv = _dot(xn, w_ref[:, A_WIDTH:A_WIDTH + 2 * KV_WIDTH])
    for t in range(KV_WIDTH // LANES):
        sl = slice(t * LANES, (t + 1) * LANES)
        k_ref[:, sl] = rope(zkv[:, sl])
    v_ref[...] = zkv[:, KV_WIDTH:]
    o = A_WIDTH + 2 * KV_WIDTH
    u_ref[...] = _gelu(_dot(xn, w_ref[:, o:o + B_WIDTH])).astype(BF16)
    zv = _gelu(_dot(xn, w_ref[:, o + B_WIDTH:o + 2 * B_WIDTH]))
    for g in range(B_GROUPS):
        sl = slice(g * GMLP_DIM, (g + 1) * GMLP_DIM)
        x = zv[:, sl]
        xc = x - jnp.mean(x, axis=-1, keepdims=True)
        y = xc * lax.rsqrt(jnp.mean(xc * xc, axis=-1, keepdims=True) + EPS)
        gv_ref[:, sl] = y * gv_g_ref[:, sl]


def _even_in(h, g, w, ca, cb, gv_g):
    n = N_TOK // TM
    row = lambda i: (i, 0)
    const = lambda i: (0, 0)
    return pl.pallas_call(
        _even_in_kernel,
        grid=(n,),
        in_specs=[
            pl.BlockSpec((TM, D_MODEL), row),
            pl.BlockSpec((1, D_MODEL), const),
            pl.BlockSpec((D_MODEL, EVEN_IN), const, pipeline_mode=pl.Buffered(1)),
            pl.BlockSpec((TM, LANES), row),
            pl.BlockSpec((TM, LANES), row),
            pl.BlockSpec((1, B_WIDTH), const),
        ],
        out_specs=[
            pl.BlockSpec((TM, A_WIDTH), row),
            pl.BlockSpec((TM, KV_WIDTH), row),
            pl.BlockSpec((TM, KV_WIDTH), row),
            pl.BlockSpec((TM, B_WIDTH), row),
            pl.BlockSpec((TM, B_WIDTH), row),
        ],
        out_shape=[
            jax.ShapeDtypeStruct((N_TOK, A_WIDTH), BF16),
            jax.ShapeDtypeStruct((N_TOK, KV_WIDTH), F32),
            jax.ShapeDtypeStruct((N_TOK, KV_WIDTH), F32),
            jax.ShapeDtypeStruct((N_TOK, B_WIDTH), BF16),
            jax.ShapeDtypeStruct((N_TOK, B_WIDTH), F32),
        ],
        compiler_params=_params(("arbitrary",)),
        name="even_in",
    )(h, g, w, ca, cb, gv_g)


N_MIX_PROMPT = N_PROMPT // MIX_TM
ITEMS_PER_TILE = MIX_TM // DEC_SEQ
N_MIX = N_TOK // MIX_TM


def _spread_heads(x, kv_head):
    t = x[:, (kv_head // 2) * LANES:(kv_head // 2 + 1) * LANES]
    lane = lax.broadcasted_iota(jnp.int32, t.shape, 1)
    if kv_head % 2 == 0:
        lo = jnp.where(lane < HEAD_DIM, t, 0.0)
        hi = pltpu.roll(lo, HEAD_DIM, 1)
    else:
        hi = jnp.where(lane >= HEAD_DIM, t, 0.0)
        lo = pltpu.roll(hi, HEAD_DIM, 1)
    return lo, hi


def _attend(q2, kk, vv, sink_a, sink_b, allowed):
    s_len = kk[0].shape[0]
    kx = jnp.concatenate(kk, axis=0).astype(BF16)
    vx = jnp.concatenate(vv, axis=0).astype(BF16)
    s = lax.dot_general(q2, kx, _NT, preferred_element_type=F32)
    ps, inv = [], []
    for half, sink in ((0, sink_a), (1, sink_b)):
        sh = s[:, half * s_len:(half + 1) * s_len]
        if allowed is not None:
            sh = jnp.where(allowed, sh, -jnp.inf)
        m = jnp.maximum(jnp.max(sh, axis=-1, keepdims=True), sink)
        p = jnp.exp(sh - m)
        inv.append(1.0 / (jnp.sum(p, axis=-1, keepdims=True) + jnp.exp(sink - m)))
        ps.append(p)
    o = _dot(jnp.concatenate(ps, axis=1).astype(BF16), vx)
    lane = lax.broadcasted_iota(jnp.int32, o.shape, 1)
    return o * jnp.where(lane < HEAD_DIM, inv[0], inv[1])


def _mixer_kernel(sink_ref, q_ref, kp_ref, kc_ref, vp_ref, vc_ref, ck_ref, cv_ref,
                  u_ref, gv_ref, wp_ref, ws_ref, bp_ref, bs_ref,
                  ab_ref, ko_ref, vo_ref):
    i = pl.program_id(0)

    def gmlp(w_ref, b_ref):
        for g in range(B_GROUPS):
            sl = slice(g * GMLP_DIM, (g + 1) * GMLP_DIM)
            mixed = _dot(w_ref[g], gv_ref[:, sl].astype(BF16)) + b_ref[:, g:g + 1]
            ab_ref[:, A_WIDTH + g * GMLP_DIM:A_WIDTH + (g + 1) * GMLP_DIM] = (
                u_ref[:, sl].astype(F32) * mixed).astype(BF16)

    @pl.when(i < N_MIX_PROMPT)
    def _prompt():
        kk = jnp.concatenate([kp_ref[...], kc_ref[...]], axis=0)
        vv = jnp.concatenate([vp_ref[...], vc_ref[...]], axis=0)
        row = lax.broadcasted_iota(jnp.int32, (MIX_TM, 2 * MIX_TM), 0)
        col = lax.broadcasted_iota(jnp.int32, (MIX_TM, 2 * MIX_TM), 1)
        allowed = ((row < CHUNK) & (col < 3 * CHUNK)) | ((row >= CHUNK) & (col >= CHUNK))
        allowed = allowed & ((col >= MIX_TM) | (i > 0))
        for j in range(A_KV_HEADS):
            ks = _spread_heads(kk, j)
            vs = _spread_heads(vv, j)
            for p in (2 * j, 2 * j + 1):
                sl = slice(p * LANES, (p + 1) * LANES)
                o = _attend(q_ref[:, sl], ks, vs, sink_ref[2 * p], sink_ref[2 * p + 1], allowed)
                ab_ref[:, sl] = o.astype(BF16)
        gmlp(wp_ref, bp_ref)

    @pl.when(i >= N_MIX_PROMPT)
    def _sample():
        for a in range(ITEMS_PER_TILE):
            rows = slice(a * DEC_SEQ, (a + 1) * DEC_SEQ)
            k_new = kc_ref[rows, :]
            v_new = vc_ref[rows, :]
            kk = jnp.concatenate([ck_ref[a], k_new], axis=0)
            vv = jnp.concatenate([cv_ref[a], v_new], axis=0)
            ko_ref[a] = kk[DEC_SEQ:, :]
            vo_ref[a] = vv[DEC_SEQ:, :]
            for j in range(A_KV_HEADS):
                ks = _spread_heads(kk, j)
                vs = _spread_heads(vv, j)
                for p in (2 * j, 2 * j + 1):
                    sl = slice(p * LANES, (p + 1) * LANES)
                    o = _attend(q_ref[rows, sl], ks, vs, sink_ref[2 * p], sink_ref[2 * p + 1], None)
                    ab_ref[rows, sl] = o.astype(BF16)
        gmlp(ws_ref, bs_ref)


def _mixer(sinks, q, k, v, cache_k, cache_v, u, gv, w_p, w_s, b_p, b_s):
    row = lambda i: (i, 0)
    prev = lambda i: (jnp.maximum(i - 1, 0), 0)
    item = lambda i: (jnp.maximum(i - N_MIX_PROMPT, 0), 0, 0)
    c2 = lambda i: (0, 0)
    c3 = lambda i: (0, 0, 0)
    cache_block = (ITEMS_PER_TILE, WINDOW, KV_WIDTH)
    return pl.pallas_call(
        _mixer_kernel,
        grid=(N_MIX,),
        in_specs=[
            pl.BlockSpec(memory_space=pltpu.SMEM),
            pl.BlockSpec((MIX_TM, A_WIDTH), row),
            pl.BlockSpec((MIX_TM, KV_WIDTH), prev),
            pl.BlockSpec((MIX_TM, KV_WIDTH), row),
            pl.BlockSpec((MIX_TM, KV_WIDTH), prev),
            pl.BlockSpec((MIX_TM, KV_WIDTH), row),
            pl.BlockSpec(cache_block, item),
            pl.BlockSpec(cache_block, item),
            pl.BlockSpec((MIX_TM, B_WIDTH), row),
            pl.BlockSpec((MIX_TM, B_WIDTH), row),
            pl.BlockSpec((B_GROUPS, MIX_TM, MIX_TM), c3),
            pl.BlockSpec((B_GROUPS, MIX_TM, MIX_TM), c3),
            pl.BlockSpec((MIX_TM, B_GROUPS), c2),
            pl.BlockSpec((MIX_TM, B_GROUPS), c2),
        ],
        out_specs=[
            pl.BlockSpec((MIX_TM, A_WIDTH + B_WIDTH), row),
            pl.BlockSpec(cache_block, item),
            pl.BlockSpec(cache_block, item),
        ],
        out_shape=[
            jax.ShapeDtypeStruct((N_TOK, A_WIDTH + B_WIDTH), BF16),
            jax.ShapeDtypeStruct((DEC_BATCH, WINDOW, KV_WIDTH), F32),
            jax.ShapeDtypeStruct((DEC_BATCH, WINDOW, KV_WIDTH), F32),
        ],
        compiler_params=_params(("arbitrary",)),
        name="mixer",
    )(sinks, q, k, k, v, v, cache_k, cache_v, u, gv, w_p, w_s, b_p, b_s)


N_FF = D_FF // FF_BLOCK


def _proj_ffn_kernel(h_ref, y_ref, wo_ref, gf_ref, wu_ref, wd_ref, gn_ref, *refs, final):
    if final:
        (o_ref, xn_ref) = refs
    else:
        (o_ref, nxt_ref, xn_ref) = refs
    f = pl.program_id(1)

    @pl.when(f == 0)
    def _():
        h1 = h_ref[...] + _dot(y_ref[...], wo_ref[...])
        o_ref[...] = h1
        xn_ref[...] = _rms(h1, gf_ref[...]).astype(BF16)

    t = jnp.maximum(_dot(xn_ref[...], wu_ref[...]), 0.0)
    o_ref[...] += _dot((t * t).astype(BF16), wd_ref[...])

    @pl.when(f == N_FF - 1)
    def _():
        nrm = _rms(o_ref[...], gn_ref[...])
        if final:
            o_ref[...] = nrm
        else:
            nxt_ref[...] = nrm.astype(BF16)


def _proj_ffn(h, y, w_out, g_ffn, w_up, w_down, g_next, final):
    row = lambda i, f: (i, 0)
    const = lambda i, f: (0, 0)
    out_specs = [pl.BlockSpec((TM, D_MODEL), row)]
    out_shape = [jax.ShapeDtypeStruct((N_TOK, D_MODEL), F32)]
    if not final:
        out_specs.append(pl.BlockSpec((TM, D_MODEL), row))
        out_shape.append(jax.ShapeDtypeStruct((N_TOK, D_MODEL), BF16))
    return pl.pallas_call(
        functools.partial(_proj_ffn_kernel, final=final),
        grid=(N_TOK // TM, N_FF),
        in_specs=[
            pl.BlockSpec((TM, D_MODEL), row),
            pl.BlockSpec((TM, D_MODEL), row),
            pl.BlockSpec((D_MODEL, D_MODEL), const, pipeline_mode=pl.Buffered(1)),
            pl.BlockSpec((1, D_MODEL), const),
            pl.BlockSpec((D_MODEL, FF_BLOCK), lambda i, f: (0, f)),
            pl.BlockSpec((FF_BLOCK, D_MODEL), lambda i, f: (f, 0)),
            pl.BlockSpec((1, D_MODEL), const),
        ],
        out_specs=out_specs,
        out_shape=out_shape,
        scratch_shapes=[pltpu.VMEM((TM, D_MODEL), BF16)],
        compiler_params=_params(("arbitrary", "arbitrary")),
        name="proj_ffn_final" if final else "proj_ffn",
    )(h, y, w_out, g_ffn, w_up, w_down, g_next)


N_HG_PROMPT = N_PROMPT // TM
N_HG = N_TOK // TM
CHUNKS_PER_TILE = TM // HG_CHUNK
ITEMS_PER_CHUNK = HG_CHUNK // DEC_SEQ


def _hgrn_scores(q, k, f, b_ref, seg):
    n = HG_CHUNK
    rows = lax.broadcasted_iota(jnp.int32, (n, n), 0)
    cols = lax.broadcasted_iota(jnp.int32, (n, n), 1)
    scores = jnp.where(rows == cols, jnp.sum(q * k, axis=1, keepdims=True), 0.0)
    h = seg // 2
    while h >= 1:
        if h >= 8:
            qs, ks = [], []
            zero = jnp.zeros((h, C_DIM), F32)
            for g in range(n // (2 * h)):
                base = g * 2 * h
                beta = b_ref[base + h - 1:base + h, :]
                lo = slice(base, base + h)
                up = slice(base + h, base + 2 * h)
                ks += [k[lo] * jnp.exp(beta - b_ref[lo, :]), zero]
                qs += [zero, q[up] * jnp.exp(b_ref[up, :] - beta)]
            qt = jnp.concatenate(qs, axis=0)
            kt = jnp.concatenate(ks, axis=0)
        elif h == 4:
            qs, ks = [], []
            upper = lax.broadcasted_iota(jnp.int32, (8, C_DIM), 0) >= 4
            for g in range(n // 8):
                sl = slice(8 * g, 8 * g + 8)
                e = jnp.exp(-jnp.abs(b_ref[sl, :] - b_ref[8 * g + 3:8 * g + 4, :]))
                qs.append(jnp.where(upper, q[sl] * e, 0.0))
                ks.append(jnp.where(upper, 0.0, k[sl] * e))
            qt = jnp.concatenate(qs, axis=0)
            kt = jnp.concatenate(ks, axis=0)
        elif h == 2:
            r4 = lax.broadcasted_iota(jnp.int32, (n, C_DIM), 0) % 4
            f_next = pltpu.roll(f, n - 1, 0)
            f_prev = pltpu.roll(f, 1, 0)
            kt = jnp.where(r4 == 0, k * f_next, jnp.where(r4 == 1, k, 0.0))
            qt = jnp.where(r4 == 2, q * f, jnp.where(r4 == 3, q * f * f_prev, 0.0))
        else:
            odd = (lax.broadcasted_iota(jnp.int32, (n, C_DIM), 0) % 2) == 1
            kt = jnp.where(odd, 0.0, k)
            qt = jnp.where(odd, q * f, 0.0)
        s_h = lax.dot_general(qt.astype(BF16), kt.astype(BF16), _NT, preferred_element_type=F32)
        if 2 * h < n:
            sh = int(math.log2(2 * h))
            s_h = jnp.where((rows >> sh) == (cols >> sh), s_h, 0.0)
        scores = scores + s_h
        h //= 2
    return scores


def _hgrn_kernel(xn_ref, w_ref, lb_ref, go_ref, trip_ref, tris_ref, s0_ref,
                 og_ref, sp_ref, ss_ref, z_ref, b_ref, st_ref):
    t = pl.program_id(1)

    lbx = lb_ref[...]
    e = jnp.exp(lbx - jnp.max(lbx, axis=0, keepdims=True))
    p = e / jnp.sum(e, axis=0, keepdims=True)
    lb = (p[0:1, :] + p[1:2, :]) - p[0:1, :]

    z_ref[...] = _dot(xn_ref[...], w_ref[...])

    @pl.when(t == 0)
    def _():
        st_ref[...] = jnp.zeros_like(st_ref)

    def chunk(c, tri_ref, seg):
        rows = slice(c * HG_CHUNK, (c + 1) * HG_CHUNK)
        zq = z_ref[rows, 0:C_DIM]
        f = lb + (1.0 - lb) * _sigmoid(z_ref[rows, C_DIM:2 * C_DIM])
        v = z_ref[rows, 2 * C_DIM:3 * C_DIM]
        gate = _sigmoid(z_ref[rows, 3 * C_DIM:4 * C_DIM])
        q = zq * _sigmoid(zq)
        k = 1.0 - f
        b_ref[...] = jnp.dot(tri_ref[...], jnp.log(f), precision=lax.Precision.HIGHEST,
                             preferred_element_type=F32)
        scores = _hgrn_scores(q, k, f, b_ref, seg)
        o = _dot(scores.astype(BF16), v.astype(BF16))
        b = b_ref[...]
        qe = (q * jnp.exp(b)).astype(BF16)
        if seg == HG_CHUNK:
            s_t = st_ref[...]
            o = o + lax.dot_general(qe, s_t.astype(BF16), _NT, preferred_element_type=F32)
            b_last = b_ref[HG_CHUNK - 1:HG_CHUNK, :]
            ke = (k * jnp.exp(b_last - b)).astype(BF16)
            st_ref[...] = s_t * jnp.exp(b_last) + _dot(v.T.astype(BF16), ke)
        else:
            outs = []
            v_t = v.T.astype(BF16)
            row = lax.broadcasted_iota(jnp.int32, (HG_CHUNK, C_DIM), 0)
            for a in range(HG_CHUNK // seg):
                item = c * (HG_CHUNK // seg) + a
                r = slice(a * seg, (a + 1) * seg)
                s_t = s0_ref[item].T
                outs.append(lax.dot_general(qe[r], s_t.astype(BF16), _NT, preferred_element_type=F32))
                b_last = b_ref[(a + 1) * seg - 1:(a + 1) * seg, :]
                own = (row >= a * seg) & (row < (a + 1) * seg)
                ke = jnp.where(own, k * jnp.exp(jnp.minimum(b_last - b, 0.0)), 0.0).astype(BF16)
                s_new = s_t * jnp.exp(b_last) + _dot(v_t, ke)
                ss_ref[item] = s_new.T
            o = o + jnp.concatenate(outs, axis=0)
        y = _rms(o, go_ref[...]) * gate
        og_ref[rows, :] = y.astype(BF16)

    @pl.when(t < N_HG_PROMPT)
    def _():
        for c in range(CHUNKS_PER_TILE):
            chunk(c, trip_ref, HG_CHUNK)

    @pl.when(t == N_HG_PROMPT - 1)
    def _():
        sp_ref[...] = st_ref[...].T

    @pl.when(t == N_HG_PROMPT)
    def _():
        for c in range(CHUNKS_PER_TILE):
            chunk(c, tris_ref, DEC_SEQ)


def _hgrn(xn, w4, lb, g_onorm, tri_p, tri_s, state0):
    c2 = lambda j, t: (0, 0)
    head_col = lambda j, t: (0, j)
    return pl.pallas_call(
        _hgrn_kernel,
        grid=(C_HEADS, N_HG),
        in_specs=[
            pl.BlockSpec((TM, D_MODEL), lambda j, t: (t, 0)),
            pl.BlockSpec((D_MODEL, 4 * C_DIM), head_col),
            pl.BlockSpec((2, C_DIM), head_col),
            pl.BlockSpec((1, C_DIM), head_col),
            pl.BlockSpec((HG_CHUNK, HG_CHUNK), c2),
            pl.BlockSpec((HG_CHUNK, HG_CHUNK), c2),
            pl.BlockSpec((DEC_BATCH, None, C_DIM, C_DIM), lambda j, t: (0, j, 0, 0)),
        ],
        out_specs=[
            pl.BlockSpec((TM, C_DIM), lambda j, t: (t, j)),
            pl.BlockSpec((None, C_DIM, C_DIM), lambda j, t: (j, 0, 0)),
            pl.BlockSpec((DEC_BATCH, None, C_DIM, C_DIM), lambda j, t: (0, j, 0, 0)),
        ],
        out_shape=[
            jax.ShapeDtypeStruct((N_TOK, C_HEADS * C_DIM), BF16),
            jax.ShapeDtypeStruct((C_HEADS, C_DIM, C_DIM), F32),
            jax.ShapeDtypeStruct((DEC_BATCH, C_HEADS, C_DIM, C_DIM), F32),
        ],
        scratch_shapes=[
            pltpu.VMEM((TM, 4 * C_DIM), F32),
            pltpu.VMEM((HG_CHUNK, C_DIM), F32),
            pltpu.VMEM((C_DIM, C_DIM), F32),
        ],
        compiler_params=_params(("arbitrary", "arbitrary")),
        name="hgrn",
    )(xn, w4, lb, g_onorm, tri_p, tri_s, state0)


def _rope_tables():
    half = ROT_DIM // 2
    inv = ROPE_THETA ** (-(jnp.arange(half, dtype=F32) * (2.0 / ROT_DIM)))
    pos = jnp.concatenate([jnp.arange(N_PROMPT, dtype=jnp.int32),
                           jnp.tile(PAST_LEN + jnp.arange(DEC_SEQ, dtype=jnp.int32), DEC_BATCH)])
    ang = pos.astype(F32)[:, None] * inv[None, :]
    cos, sin = jnp.cos(ang), jnp.sin(ang)
    ones = jnp.ones((N_TOK, HEAD_DIM - ROT_DIM), F32)
    ca = jnp.concatenate([cos, cos, ones], axis=1)
    cb = jnp.concatenate([-sin, sin, 0.0 * ones], axis=1)
    return jnp.tile(ca, (1, LANES // HEAD_DIM)), jnp.tile(cb, (1, LANES // HEAD_DIM))


def kernel(x_prompt, x_sample, cache_swa_k, cache_swa_v, state_hgrn, norm_mix, norm_ffn, norm_final,
           w_in_even, w_out_even, attn_sinks, gmlp_vnorm, gmlp_ws, gmlp_bias, w_in_odd, w_out_odd,
           hgrn_lb, hgrn_onorm, w_ffn_up, w_ffn_down):
    h0 = jnp.concatenate([x_prompt.reshape(N_PROMPT, D_MODEL), x_sample.reshape(N_SAMPLE, D_MODEL)], axis=0)
    ca, cb = _rope_tables()

    q, k, v, u, gv = _even_in(h0, norm_mix[0:1], w_in_even[0].astype(BF16), ca, cb, gmlp_vnorm[0:1])
    ws = gmlp_ws[0]
    cidx = jnp.arange(MIX_TM) // CHUNK
    w_p = jnp.where(cidx[:, None] >= cidx[None, :], ws, 0.0).astype(BF16)
    eye = jnp.eye(ITEMS_PER_TILE, dtype=F32)
    w_s = jnp.einsum('ab,gpq->gapbq', eye, ws[:, :DEC_SEQ, :DEC_SEQ]).reshape(B_GROUPS, MIX_TM, MIX_TM).astype(BF16)
    b_p = gmlp_bias[0].T
    b_s = jnp.tile(gmlp_bias[0][:, :DEC_SEQ], (1, ITEMS_PER_TILE)).T
    ab, k_s, v_s = _mixer(attn_sinks[0], q, k, v,
                          cache_swa_k[0].reshape(DEC_BATCH, WINDOW, KV_WIDTH),
                          cache_swa_v[0].reshape(DEC_BATCH, WINDOW, KV_WIDTH),
                          u, gv, w_p, w_s, b_p, b_s)
    h1, xn1 = _proj_ffn(h0, ab, w_out_even[0].astype(BF16), norm_ffn[0:1],
                        w_ffn_up[0].astype(BF16), w_ffn_down[0].astype(BF16), norm_mix[1:2], final=False)

    w4 = w_in_odd[0].reshape(D_MODEL, 4, C_HEADS, C_DIM).transpose(0, 2, 1, 3).reshape(D_MODEL, 4 * C_HEADS * C_DIM)
    tri_p = jnp.tril(jnp.ones((HG_CHUNK, HG_CHUNK), F32))
    seg = jnp.arange(HG_CHUNK) // DEC_SEQ
    tri_s = jnp.where(seg[:, None] == seg[None, :], tri_p, 0.0)
    og, s_p, s_s = _hgrn(xn1, w4.astype(BF16), hgrn_lb, hgrn_onorm[0:1], tri_p, tri_s, state_hgrn[0])
    y, = _proj_ffn(h1, og, w_out_odd[0].astype(BF16), norm_ffn[1:2],
                   w_ffn_up[1].astype(BF16), w_ffn_down[1].astype(BF16), norm_final[None, :], final=True)

    kv_shape = (1, 1, WINDOW, A_KV_HEADS, HEAD_DIM)
    kv_s_shape = (1, DEC_BATCH, WINDOW, A_KV_HEADS, HEAD_DIM)
    return (y[:N_PROMPT].reshape(1, N_PROMPT, D_MODEL),
            y[N_PROMPT:].reshape(DEC_BATCH, DEC_SEQ, D_MODEL),
            k[N_PROMPT - WINDOW:N_PROMPT].reshape(kv_shape),
            v[N_PROMPT - WINDOW:N_PROMPT].reshape(kv_shape),
            k_s.reshape(kv_s_shape),
            v_s.reshape(kv_s_shape),
            gv[N_PROMPT:].reshape(1, DEC_BATCH, DEC_SEQ, B_GROUPS, GMLP_DIM),
            s_p[None, None],
            s_s[None])
```

```python
import functools
import math

import jax
import jax.numpy as jnp
from jax import lax
from jax.experimental import pallas as pl
from jax.experimental.pallas import tpu as pltpu

F32 = jnp.float32
BF16 = jnp.bfloat16

D_MODEL = 2048
N_PROMPT = 8192
DEC_BATCH = 16
DEC_SEQ = 32
N_SAMPLE = DEC_BATCH * DEC_SEQ
N_TOK = N_PROMPT + N_SAMPLE
PAST_LEN = 2048
EPS = 1e-6

HEAD_DIM = 64
A_HEADS = 16
A_KV_HEADS = 4
A_WIDTH = A_HEADS * HEAD_DIM
KV_WIDTH = A_KV_HEADS * HEAD_DIM
ROT_DIM = 16
ROPE_THETA = 500000.0
WINDOW = 128
CHUNK = 64
B_GROUPS = 8
GMLP_DIM = 128
B_WIDTH = B_GROUPS * GMLP_DIM
EVEN_IN = A_WIDTH + 2 * KV_WIDTH + 2 * B_WIDTH
C_HEADS = 16
C_DIM = 128
D_FF = 4 * D_MODEL

LANES = 128
TM = 512
TMB = 1088
ROW_CHUNK = 272
MIX_TM = 128
HG_CHUNK = 128
IN_BLOCK = 512
PJ_BLOCK = 256
FF_BLOCK = 512
MXU_N = 256
VMEM_LIMIT = 58 * 1024 * 1024

_NT = (((1,), (1,)), ((), ()))


def _params(sem):
    return pltpu.CompilerParams(dimension_semantics=sem, vmem_limit_bytes=VMEM_LIMIT)


def _rms(x, g):
    return x * lax.rsqrt(jnp.mean(x * x, axis=-1, keepdims=True) + EPS) * g


def _sigmoid(x):
    return 1.0 / (1.0 + jnp.exp(-x))


def _gelu(x):
    c = math.sqrt(2.0 / math.pi)
    return x * (0.5 * (1.0 + jnp.tanh(c * (x + 0.044715 * (x * x * x)))))


def _dot(a, b):
    return jnp.dot(a, b, preferred_element_type=F32)


def _even_in_kernel(h_ref, g_ref, w_ref, ca_ref, cb_ref, gv_g_ref,
                    q_ref, k_ref, v_ref, u_ref, gv_ref, xn_ref):
    j = pl.program_id(1)

    @pl.when(j == 0)
    def _():
        for r in range(0, TMB, ROW_CHUNK):
            rows = slice(r, r + ROW_CHUNK)
            xn_ref[rows, :] = _rms(h_ref[rows, :], g_ref[...]).astype(BF16)

    z = _dot(xn_ref[...], w_ref[...].astype(BF16))

    def rope(x):
        lane = lax.broadcasted_iota(jnp.int32, (TMB, LANES), 1)
        first = (lane & (HEAD_DIM - 1)) < (ROT_DIM // 2)
        partner = jnp.where(first, pltpu.roll(x, LANES - ROT_DIM // 2, 1), pltpu.roll(x, ROT_DIM // 2, 1))
        return x * ca_ref[...] + partner * cb_ref[...]

    tiles = [slice(t * LANES, (t + 1) * LANES) for t in range(IN_BLOCK // LANES)]

    @pl.when(j < A_WIDTH // IN_BLOCK)
    def _():
        for sl in tiles:
            q_ref[:, sl] = (rope(z[:, sl]) * (HEAD_DIM ** -0.5)).astype(BF16)

    @pl.when(j == A_WIDTH // IN_BLOCK)
    def _():
        for sl in tiles[:KV_WIDTH // LANES]:
            k_ref[:, sl] = rope(z[:, sl])
        v_ref[...] = z[:, KV_WIDTH:]

    @pl.when((j > A_WIDTH // IN_BLOCK) & (j <= (A_WIDTH + B_WIDTH) // IN_BLOCK))
    def _():
        u_ref[...] = _gelu(z).astype(BF16)

    @pl.when(j > (A_WIDTH + B_WIDTH) // IN_BLOCK)
    def _():
        for sl in tiles:
            x = _gelu(z[:, sl])
            xc = x - jnp.mean(x, axis=-1, keepdims=True)
            y = xc * lax.rsqrt(jnp.mean(xc * xc, axis=-1, keepdims=True) + EPS)
            gv_ref[:, sl] = y * gv_g_ref[:, sl]


def _even_in(h, g, w, ca, cb, gv_g):
    nq = A_WIDTH // IN_BLOCK
    nu = nq + 1
    nv = nu + B_WIDTH // IN_BLOCK
    row = lambda i, j: (i, 0)
    const = lambda i, j: (0, 0)
    return pl.pallas_call(
        _even_in_kernel,
        grid=(N_TOK // TMB, EVEN_IN // IN_BLOCK),
        in_specs=[
            pl.BlockSpec((TMB, D_MODEL), row),
            pl.BlockSpec((1, D_MODEL), const),
            pl.BlockSpec((D_MODEL, IN_BLOCK), lambda i, j: (0, j)),
            pl.BlockSpec((TMB, LANES), row),
            pl.BlockSpec((TMB, LANES), row),
            pl.BlockSpec((1, IN_BLOCK), lambda i, j: (0, jnp.maximum(j - nv, 0))),
        ],
        out_specs=[
            pl.BlockSpec((TMB, IN_BLOCK), lambda i, j: (i, jnp.minimum(j, nq - 1))),
            pl.BlockSpec((TMB, KV_WIDTH), row),
            pl.BlockSpec((TMB, KV_WIDTH), row),
            pl.BlockSpec((TMB, IN_BLOCK), lambda i, j: (i, jnp.clip(j - nu, 0, nv - nu - 1))),
            pl.BlockSpec((TMB, IN_BLOCK), lambda i, j: (i, jnp.maximum(j - nv, 0))),
        ],
        out_shape=[
            jax.ShapeDtypeStruct((N_TOK, A_WIDTH), BF16),
            jax.ShapeDtypeStruct((N_TOK, KV_WIDTH), F32),
            jax.ShapeDtypeStruct((N_TOK, KV_WIDTH), F32),
            jax.ShapeDtypeStruct((N_TOK, B_WIDTH), BF16),
            jax.ShapeDtypeStruct((N_TOK, B_WIDTH), F32),
        ],
        scratch_shapes=[pltpu.VMEM((TMB, D_MODEL), BF16)],
        compiler_params=_params(("arbitrary", "arbitrary")),
        name="even_in",
    )(h, g, w, ca, cb, gv_g)


N_MIX_PROMPT = N_PROMPT // MIX_TM
ITEMS_PER_TILE = MIX_TM // DEC_SEQ
N_MIX = N_TOK // MIX_TM


def _spread_heads(x, kv_head):
    t = x[:, (kv_head // 2) * LANES:(kv_head // 2 + 1) * LANES]
    lane = lax.broadcasted_iota(jnp.int32, t.shape, 1)
    if kv_head % 2 == 0:
        lo = jnp.where(lane < HEAD_DIM, t, 0.0)
        hi = pltpu.roll(lo, HEAD_DIM, 1)
    else:
        hi = jnp.where(lane >= HEAD_DIM, t, 0.0)
        lo = pltpu.roll(hi, HEAD_DIM, 1)
    return lo, hi


def _attend(q2, kk, vv, sink_a, sink_b, allowed):
    s_len = kk[0].shape[0]
    kx = jnp.concatenate(kk, axis=0).astype(BF16)
    vx = jnp.concatenate(vv, axis=0).astype(BF16)
    s = lax.dot_general(q2, kx, _NT, preferred_element_type=F32)
    ps, inv = [], []
    for half, sink in ((0, sink_a), (1, sink_b)):
        sh = s[:, half * s_len:(half + 1) * s_len]
        if allowed is not None:
            sh = jnp.where(allowed, sh, -jnp.inf)
        m = jnp.maximum(jnp.max(sh, axis=-1, keepdims=True), sink)
        p = jnp.exp(sh - m)
        inv.append(1.0 / (jnp.sum(p, axis=-1, keepdims=True) + jnp.exp(sink - m)))
        ps.append(p)
    o = _dot(jnp.concatenate(ps, axis=1).astype(BF16), vx)
    lane = lax.broadcasted_iota(jnp.int32, o.shape, 1)
    return o * jnp.where(lane < HEAD_DIM, inv[0], inv[1])


def _mixer_kernel(sink_ref, q_ref, kp_ref, kc_ref, vp_ref, vc_ref, ck_ref, cv_ref,
                  u_ref, gv_ref, wp_ref, ws_ref, bp_ref, bs_ref,
                  ab_ref, ko_ref, vo_ref):
    i = pl.program_id(0)

    def gmlp(w_ref, b_ref):
        for g in range(B_GROUPS):
            sl = slice(g * GMLP_DIM, (g + 1) * GMLP_DIM)
            mixed = _dot(w_ref[g], gv_ref[:, sl].astype(BF16)) + b_ref[:, g:g + 1]
            ab_ref[:, A_WIDTH + g * GMLP_DIM:A_WIDTH + (g + 1) * GMLP_DIM] = (
                u_ref[:, sl].astype(F32) * mixed).astype(BF16)

    @pl.when(i < N_MIX_PROMPT)
    def _prompt():
        kk = jnp.concatenate([kp_ref[...], kc_ref[...]], axis=0)
        vv = jnp.concatenate([vp_ref[...], vc_ref[...]], axis=0)
        row = lax.broadcasted_iota(jnp.int32, (MIX_TM, 2 * MIX_TM), 0)
        col = lax.broadcasted_iota(jnp.int32, (MIX_TM, 2 * MIX_TM), 1)
        allowed = ((row < CHUNK) & (col < 3 * CHUNK)) | ((row >= CHUNK) & (col >= CHUNK))
        allowed = allowed & ((col >= MIX_TM) | (i > 0))
        for j in range(A_KV_HEADS):
            ks = _spread_heads(kk, j)
            vs = _spread_heads(vv, j)
            for p in (2 * j, 2 * j + 1):
                sl = slice(p * LANES, (p + 1) * LANES)
                o = _attend(q_ref[:, sl], ks, vs, sink_ref[2 * p], sink_ref[2 * p + 1], allowed)
                ab_ref[:, sl] = o.astype(BF16)
        gmlp(wp_ref, bp_ref)

    @pl.when(i >= N_MIX_PROMPT)
    def _sample():
        for a in range(ITEMS_PER_TILE):
            rows = slice(a * DEC_SEQ, (a + 1) * DEC_SEQ)
            k_new = kc_ref[rows, :]
            v_new = vc_ref[rows, :]
            kk = jnp.concatenate([ck_ref[a], k_new], axis=0)
            vv = jnp.concatenate([cv_ref[a], v_new], axis=0)
            ko_ref[a] = kk[DEC_SEQ:, :]
            vo_ref[a] = vv[DEC_SEQ:, :]
            for j in range(A_KV_HEADS):
                ks = _spread_heads(kk, j)
                vs = _spread_heads(vv, j)
                for p in (2 * j, 2 * j + 1):
                    sl = slice(p * LANES, (p + 1) * LANES)
                    o = _attend(q_ref[rows, sl], ks, vs, sink_ref[2 * p], sink_ref[2 * p + 1], None)
                    ab_ref[rows, sl] = o.astype(BF16)
        gmlp(ws_ref, bs_ref)


def _mixer(sinks, q, k, v, cache_k, cache_v, u, gv, w_p, w_s, b_p, b_s):
    row = lambda i: (i, 0)
    prev = lambda i: (jnp.maximum(i - 1, 0), 0)
    item = lambda i: (jnp.maximum(i - N_MIX_PROMPT, 0), 0, 0)
    c2 = lambda i: (0, 0)
    c3 = lambda i: (0, 0, 0)
    cache_block = (ITEMS_PER_TILE, WINDOW, KV_WIDTH)
    return pl.pallas_call(
        _mixer_kernel,
        grid=(N_MIX,),
        in_specs=[
            pl.BlockSpec(memory_space=pltpu.SMEM),
            pl.BlockSpec((MIX_TM, A_WIDTH), row),
            pl.BlockSpec((MIX_TM, KV_WIDTH), prev),
            pl.BlockSpec((MIX_TM, KV_WIDTH), row),
            pl.BlockSpec((MIX_TM, KV_WIDTH), prev),
            pl.BlockSpec((MIX_TM, KV_WIDTH), row),
            pl.BlockSpec(cache_block, item),
            pl.BlockSpec(cache_block, item),
            pl.BlockSpec((MIX_TM, B_WIDTH), row),
            pl.BlockSpec((MIX_TM, B_WIDTH), row),
            pl.BlockSpec((B_GROUPS, MIX_TM, MIX_TM), c3),
            pl.BlockSpec((B_GROUPS, MIX_TM, MIX_TM), c3),
            pl.BlockSpec((MIX_TM, B_GROUPS), c2),
            pl.BlockSpec((MIX_TM, B_GROUPS), c2),
        ],
        out_specs=[
            pl.BlockSpec((MIX_TM, A_WIDTH + B_WIDTH), row),
            pl.BlockSpec(cache_block, item),
            pl.BlockSpec(cache_block, item),
        ],
        out_shape=[
            jax.ShapeDtypeStruct((N_TOK, A_WIDTH + B_WIDTH), BF16),
            jax.ShapeDtypeStruct((DEC_BATCH, WINDOW, KV_WIDTH), F32),
            jax.ShapeDtypeStruct((DEC_BATCH, WINDOW, KV_WIDTH), F32),
        ],
        compiler_params=_params(("arbitrary",)),
        name="mixer",
    )(sinks, q, k, k, v, v, cache_k, cache_v, u, gv, w_p, w_s, b_p, b_s)


N_PJ = D_MODEL // PJ_BLOCK
N_FF = D_FF // FF_BLOCK


def _proj_ffn_kernel(h_ref, y_ref, wo_ref, gf_ref, wu_ref, wd_ref, gn_ref, *refs, final):
    if final:
        (o_ref, xn_ref) = refs
    else:
        (o_ref, nxt_ref, xn_ref) = refs
    s = pl.program_id(1)
    cols = [slice(c * MXU_N, (c + 1) * MXU_N) for c in range(D_MODEL // MXU_N)]
    row_chunks = [slice(r, r + ROW_CHUNK) for r in range(0, TMB, ROW_CHUNK)]

    def accumulate(lhs, w):
        for sl in cols:
            o_ref[:, sl] += _dot(lhs, w[:, sl])

    @pl.when(s == 0)
    def _():
        o_ref[...] = jnp.zeros_like(o_ref)

    @pl.when(s < N_PJ)
    def _():
        accumulate(y_ref[...], wo_ref[...].astype(BF16))

    for c in range(N_PJ):
        @pl.when(s == c)
        def _():
            sl = slice(c * PJ_BLOCK, (c + 1) * PJ_BLOCK)
            o_ref[:, sl] += h_ref[...]

    @pl.when(s == N_PJ)
    def _():
        for rows in row_chunks:
            xn_ref[rows, :] = _rms(o_ref[rows, :], gf_ref[...]).astype(BF16)

    @pl.when(s >= N_PJ)
    def _():
        t = jnp.maximum(_dot(xn_ref[...], wu_ref[...].astype(BF16)), 0.0)
        accumulate((t * t).astype(BF16), wd_ref[...].astype(BF16))

    @pl.when(s == N_PJ + N_FF - 1)
    def _():
        for rows in row_chunks:
            nrm = _rms(o_ref[rows, :], gn_ref[...])
            if final:
                o_ref[rows, :] = nrm
            else:
                nxt_ref[rows, :] = nrm.astype(BF16)


def _proj_ffn(h, y, w_out, g_ffn, w_up, w_down, layer, g_next, final):
    row = lambda i, s: (i, 0)
    const = lambda i, s: (0, 0)
    pj = lambda s: jnp.minimum(s, N_PJ - 1)
    ff = lambda s: jnp.maximum(s - N_PJ, 0)
    out_specs = [pl.BlockSpec((TMB, D_MODEL), row)]
    out_shape = [jax.ShapeDtypeStruct((N_TOK, D_MODEL), F32)]
    if not final:
        out_specs.append(pl.BlockSpec((TMB, D_MODEL), row))
        out_shape.append(jax.ShapeDtypeStruct((N_TOK, D_MODEL), BF16))
    return pl.pallas_call(
        functools.partial(_proj_ffn_kernel, final=final),
        grid=(N_TOK // TMB, N_PJ + N_FF),
        in_specs=[
            pl.BlockSpec((TMB, PJ_BLOCK), lambda i, s: (i, pj(s))),
            pl.BlockSpec((TMB, PJ_BLOCK), lambda i, s: (i, pj(s))),
            pl.BlockSpec((PJ_BLOCK, D_MODEL), lambda i, s: (pj(s), 0)),
            pl.BlockSpec((1, D_MODEL), const),
            pl.BlockSpec((None, D_MODEL, FF_BLOCK), lambda i, s: (layer, 0, ff(s))),
            pl.BlockSpec((None, FF_BLOCK, D_MODEL), lambda i, s: (layer, ff(s), 0)),
            pl.BlockSpec((1, D_MODEL), const),
        ],
        out_specs=out_specs,
        out_shape=out_shape,
        scratch_shapes=[pltpu.VMEM((TMB, D_MODEL), BF16)],
        compiler_params=_params(("arbitrary", "arbitrary")),
        name="proj_ffn_final" if final else "proj_ffn",
    )(h, y, w_out, g_ffn, w_up, w_down, g_next)


N_HG_PROMPT = N_PROMPT // TM
N_HG = N_TOK // TM
CHUNKS_PER_TILE = TM // HG_CHUNK
ITEMS_PER_CHUNK = HG_CHUNK // DEC_SEQ


def _hgrn_scores(q, k, f, b_ref, seg):
    n = HG_CHUNK
    rows = lax.broadcasted_iota(jnp.int32, (n, n), 0)
    cols = lax.broadcasted_iota(jnp.int32, (n, n), 1)
    scores = jnp.where(rows == cols, jnp.sum(q * k, axis=1, keepdims=True), 0.0)
    h = seg // 2
    while h >= 1:
        if h >= 8:
            qs, ks = [], []
            zero = jnp.zeros((h, C_DIM), F32)
            for g in range(n // (2 * h)):
                base = g * 2 * h
                beta = b_ref[base + h - 1:base + h, :]
                lo = slice(base, base + h)
                up = slice(base + h, base + 2 * h)
                ks += [k[lo] * jnp.exp(beta - b_ref[lo, :]), zero]
                qs += [zero, q[up] * jnp.exp(b_ref[up, :] - beta)]
            qt = jnp.concatenate(qs, axis=0)
            kt = jnp.concatenate(ks, axis=0)
        elif h == 4:
            qs, ks = [], []
            upper = lax.broadcasted_iota(jnp.int32, (8, C_DIM), 0) >= 4
            for g in range(n // 8):
                sl = slice(8 * g, 8 * g + 8)
                e = jnp.exp(-jnp.abs(b_ref[sl, :] - b_ref[8 * g + 3:8 * g + 4, :]))
                qs.append(jnp.where(upper, q[sl] * e, 0.0))
                ks.append(jnp.where(upper, 0.0, k[sl] * e))
            qt = jnp.concatenate(qs, axis=0)
            kt = jnp.concatenate(ks, axis=0)
        elif h == 2:
            r4 = lax.broadcasted_iota(jnp.int32, (n, C_DIM), 0) % 4
            f_next = pltpu.roll(f, n - 1, 0)
            f_prev = pltpu.roll(f, 1, 0)
            kt = jnp.where(r4 == 0, k * f_next, jnp.where(r4 == 1, k, 0.0))
            qt = jnp.where(r4 == 2, q * f, jnp.where(r4 == 3, q * f * f_prev, 0.0))
        else:
            odd = (lax.broadcasted_iota(jnp.int32, (n, C_DIM), 0) % 2) == 1
            kt = jnp.where(odd, 0.0, k)
            qt = jnp.where(odd, q * f, 0.0)
        s_h = lax.dot_general(qt.astype(BF16), kt.astype(BF16), _NT, preferred_element_type=F32)
        if 2 * h < n:
            sh = int(math.log2(2 * h))
            s_h = jnp.where((rows >> sh) == (cols >> sh), s_h, 0.0)
        scores = scores + s_h
        h //= 2
    return scores


def _hgrn_kernel(xn_ref, wq_ref, wf_ref, wi_ref, wg_ref, lb_ref, go_ref, trip_ref, tris_ref, s0_ref,
                 og_ref, sp_ref, ss_ref, w_ref, z_ref, b_ref, st_ref):
    t = pl.program_id(1)

    lbx = lb_ref[...]
    e = jnp.exp(lbx - jnp.max(lbx, axis=0, keepdims=True))
    p = e / jnp.sum(e, axis=0, keepdims=True)
    lb = (p[0:1, :] + p[1:2, :]) - p[0:1, :]

    @pl.when(t == 0)
    def _():
        st_ref[...] = jnp.zeros_like(st_ref)
        for c, src in enumerate((wq_ref, wf_ref, wi_ref, wg_ref)):
            w_ref[:, c * C_DIM:(c + 1) * C_DIM] = src[...].astype(BF16)

    z_ref[...] = _dot(xn_ref[...], w_ref[...])

    def chunk(c, tri_ref, seg):
        rows = slice(c * HG_CHUNK, (c + 1) * HG_CHUNK)
        zq = z_ref[rows, 0:C_DIM]
        f = lb + (1.0 - lb) * _sigmoid(z_ref[rows, C_DIM:2 * C_DIM])
        v = z_ref[rows, 2 * C_DIM:3 * C_DIM]
        gate = _sigmoid(z_ref[rows, 3 * C_DIM:4 * C_DIM])
        q = zq * _sigmoid(zq)
        k = 1.0 - f
        b_ref[...] = jnp.dot(tri_ref[...], jnp.log(f), precision=lax.Precision.HIGHEST,
                             preferred_element_type=F32)
        scores = _hgrn_scores(q, k, f, b_ref, seg)
        o = _dot(scores.astype(BF16), v.astype(BF16))
        b = b_ref[...]
        qe = (q * jnp.exp(b)).astype(BF16)
        if seg == HG_CHUNK:
            s_t = st_ref[...]
            o = o + lax.dot_general(qe, s_t.astype(BF16), _NT, preferred_element_type=F32)
            b_last = b_ref[HG_CHUNK - 1:HG_CHUNK, :]
            ke = (k * jnp.exp(b_last - b)).astype(BF16)
            st_ref[...] = s_t * jnp.exp(b_last) + _dot(v.T.astype(BF16), ke)
        else:
            outs = []
            v_t = v.T.astype(BF16)
            row = lax.broadcasted_iota(jnp.int32, (HG_CHUNK, C_DIM), 0)
            for a in range(HG_CHUNK // seg):
                item = c * (HG_CHUNK // seg) + a
                r = slice(a * seg, (a + 1) * seg)
                s_t = s0_ref[item].T
                outs.append(lax.dot_general(qe[r], s_t.astype(BF16), _NT, preferred_element_type=F32))
                b_last = b_ref[(a + 1) * seg - 1:(a + 1) * seg, :]
                own = (row >= a * seg) & (row < (a + 1) * seg)
                ke = jnp.where(own, k * jnp.exp(jnp.minimum(b_last - b, 0.0)), 0.0).astype(BF16)
                s_new = s_t * jnp.exp(b_last) + _dot(v_t, ke)
                ss_ref[item] = s_new.T
            o = o + jnp.concatenate(outs, axis=0)
        y = _rms(o, go_ref[...]) * gate
        og_ref[rows, :] = y.astype(BF16)

    @pl.when(t < N_HG_PROMPT)
    def _():
        for c in range(CHUNKS_PER_TILE):
            chunk(c, trip_ref, HG_CHUNK)

    @pl.when(t == N_HG_PROMPT - 1)
    def _():
        sp_ref[...] = st_ref[...].T

    @pl.when(t == N_HG_PROMPT)
    def _():
        for c in range(CHUNKS_PER_TILE):
            chunk(c, tris_ref, DEC_SEQ)


def _hgrn(xn, w_in, lb, g_onorm, tri_p, tri_s, state0):
    c2 = lambda j, t: (0, 0)
    head_col = lambda j, t: (0, j)
    w_specs = [pl.BlockSpec((D_MODEL, C_DIM), functools.partial(lambda j, t, c: (0, c * C_HEADS + j), c=c))
               for c in range(4)]
    return pl.pallas_call(
        _hgrn_kernel,
        grid=(C_HEADS, N_HG),
        in_specs=[
            pl.BlockSpec((TM, D_MODEL), lambda j, t: (t, 0)),
            *w_specs,
            pl.BlockSpec((2, C_DIM), head_col),
            pl.BlockSpec((1, C_DIM), head_col),
            pl.BlockSpec((HG_CHUNK, HG_CHUNK), c2),
            pl.BlockSpec((HG_CHUNK, HG_CHUNK), c2),
            pl.BlockSpec((DEC_BATCH, None, C_DIM, C_DIM), lambda j, t: (0, j, 0, 0)),
        ],
        out_specs=[
            pl.BlockSpec((TM, C_DIM), lambda j, t: (t, j)),
            pl.BlockSpec((None, C_DIM, C_DIM), lambda j, t: (j, 0, 0)),
            pl.BlockSpec((DEC_BATCH, None, C_DIM, C_DIM), lambda j, t: (0, j, 0, 0)),
        ],
        out_shape=[
            jax.ShapeDtypeStruct((N_TOK, C_HEADS * C_DIM), BF16),
            jax.ShapeDtypeStruct((C_HEADS, C_DIM, C_DIM), F32),
            jax.ShapeDtypeStruct((DEC_BATCH, C_HEADS, C_DIM, C_DIM), F32),
        ],
        scratch_shapes=[
            pltpu.VMEM((D_MODEL, 4 * C_DIM), BF16),
            pltpu.VMEM((TM, 4 * C_DIM), F32),
            pltpu.VMEM((HG_CHUNK, C_DIM), F32),
            pltpu.VMEM((C_DIM, C_DIM), F32),
        ],
        compiler_params=_params(("arbitrary", "arbitrary")),
        name="hgrn",
    )(xn, w_in, w_in, w_in, w_in, lb, g_onorm, tri_p, tri_s, state0)


def _rope_tables():
    half = ROT_DIM // 2
    inv = ROPE_THETA ** (-(jnp.arange(half, dtype=F32) * (2.0 / ROT_DIM)))
    pos = jnp.concatenate([jnp.arange(N_PROMPT, dtype=jnp.int32),
                           jnp.tile(PAST_LEN + jnp.arange(DEC_SEQ, dtype=jnp.int32), DEC_BATCH)])
    ang = pos.astype(F32)[:, None] * inv[None, :]
    cos, sin = jnp.cos(ang), jnp.sin(ang)
    ones = jnp.ones((N_TOK, HEAD_DIM - ROT_DIM), F32)
    ca = jnp.concatenate([cos, cos, ones], axis=1)
    cb = jnp.concatenate([-sin, sin, 0.0 * ones], axis=1)
    return jnp.tile(ca, (1, LANES // HEAD_DIM)), jnp.tile(cb, (1, LANES // HEAD_DIM))


def kernel(x_prompt, x_sample, cache_swa_k, cache_swa_v, state_hgrn, norm_mix, norm_ffn, norm_final,
           w_in_even, w_out_even, attn_sinks, gmlp_vnorm, gmlp_ws, gmlp_bias, w_in_odd, w_out_odd,
           hgrn_lb, hgrn_onorm, w_ffn_up, w_ffn_down):
    h0 = jnp.concatenate([x_prompt.reshape(N_PROMPT, D_MODEL), x_sample.reshape(N_SAMPLE, D_MODEL)], axis=0)
    ca, cb = _rope_tables()

    q, k, v, u, gv = _even_in(h0, norm_mix[0:1], w_in_even[0], ca, cb, gmlp_vnorm[0:1])
    ws = gmlp_ws[0]
    cidx = jnp.arange(MIX_TM) // CHUNK
    w_p = jnp.where(cidx[:, None] >= cidx[None, :], ws, 0.0).astype(BF16)
    eye = jnp.eye(ITEMS_PER_TILE, dtype=F32)
    w_s = jnp.einsum('ab,gpq->gapbq', eye, ws[:, :DEC_SEQ, :DEC_SEQ]).reshape(B_GROUPS, MIX_TM, MIX_TM).astype(BF16)
    b_p = gmlp_bias[0].T
    b_s = jnp.tile(gmlp_bias[0][:, :DEC_SEQ], (1, ITEMS_PER_TILE)).T
    ab, k_s, v_s = _mixer(attn_sinks[0], q, k, v,
                          cache_swa_k[0].reshape(DEC_BATCH, WINDOW, KV_WIDTH),
                          cache_swa_v[0].reshape(DEC_BATCH, WINDOW, KV_WIDTH),
                          u, gv, w_p, w_s, b_p, b_s)
    h1, xn1 = _proj_ffn(h0, ab, w_out_even[0], norm_ffn[0:1], w_ffn_up, w_ffn_down, 0,
                        norm_mix[1:2], final=False)

    tri_p = jnp.tril(jnp.ones((HG_CHUNK, HG_CHUNK), F32))
    seg = jnp.arange(HG_CHUNK) // DEC_SEQ
    tri_s = jnp.where(seg[:, None] == seg[None, :], tri_p, 0.0)
    og, s_p, s_s = _hgrn(xn1, w_in_odd[0], hgrn_lb, hgrn_onorm[0:1], tri_p, tri_s, state_hgrn[0])
    y, = _proj_ffn(h1, og, w_out_odd[0], norm_ffn[1:2], w_ffn_up, w_ffn_down, 1,
                   norm_final[None, :], final=True)

    kv_shape = (1, 1, WINDOW, A_KV_HEADS, HEAD_DIM)
    kv_s_shape = (1, DEC_BATCH, WINDOW, A_KV_HEADS, HEAD_DIM)
    return (y[:N_PROMPT].reshape(1, N_PROMPT, D_MODEL),
            y[N_PROMPT:].reshape(DEC_BATCH, DEC_SEQ, D_MODEL),
            k[N_PROMPT - WINDOW:N_PROMPT].reshape(kv_shape),
            v[N_PROMPT - WINDOW:N_PROMPT].reshape(kv_shape),
            k_s.reshape(kv_s_shape),
            v_s.reshape(kv_s_shape),
            gv[N_PROMPT:].reshape(1, DEC_BATCH, DEC_SEQ, B_GROUPS, GMLP_DIM),
            s_p[None, None],
            s_s[None])
```

```python
import functools
import math

import jax
import jax.numpy as jnp
from jax import lax
from jax.experimental import pallas as pl
from jax.experimental.pallas import tpu as pltpu

F32 = jnp.float32
BF16 = jnp.bfloat16

D_MODEL = 2048
N_PROMPT = 8192
DEC_BATCH = 16
DEC_SEQ = 32
N_SAMPLE = DEC_BATCH * DEC_SEQ
N_TOK = N_PROMPT + N_SAMPLE
PAST_LEN = 2048
EPS = 1e-6

HEAD_DIM = 64
A_HEADS = 16
A_KV_HEADS = 4
A_WIDTH = A_HEADS * HEAD_DIM
KV_WIDTH = A_KV_HEADS * HEAD_DIM
ROT_DIM = 16
ROPE_THETA = 500000.0
WINDOW = 128
CHUNK = 64
B_GROUPS = 8
GMLP_DIM = 128
B_WIDTH = B_GROUPS * GMLP_DIM
EVEN_IN = A_WIDTH + 2 * KV_WIDTH + 2 * B_WIDTH
C_HEADS = 16
C_DIM = 128
D_FF = 4 * D_MODEL

LANES = 128
TM = 512
TMB = 1088
ROW_CHUNK = 272
MIX_TM = 128
HG_CHUNK = 128
IN_BLOCK = 512
PJ_BLOCK = 256
FF_BLOCK = 512
MXU_N = 256
VMEM_LIMIT = 58 * 1024 * 1024

_NT = (((1,), (1,)), ((), ()))


def _params(sem):
    return pltpu.CompilerParams(dimension_semantics=sem, vmem_limit_bytes=VMEM_LIMIT)


def _rms(x, g):
    return x * lax.rsqrt(jnp.mean(x * x, axis=-1, keepdims=True) + EPS) * g


def _sigmoid(x):
    return 0.5 * jnp.tanh(0.5 * x) + 0.5


def _gelu(x):
    c = math.sqrt(2.0 / math.pi)
    return x * (0.5 * (1.0 + jnp.tanh(c * (x + 0.044715 * (x * x * x)))))


def _dot(a, b):
    return jnp.dot(a, b, preferred_element_type=F32)


def _even_in_kernel(h_ref, g_ref, w_ref, ca_ref, cb_ref, gv_g_ref,
                    q_ref, k_ref, v_ref, u_ref, gv_ref, xn_ref):
    j = pl.program_id(1)

    @pl.when(j == 0)
    def _():
        for r in range(0, TMB, ROW_CHUNK):
            rows = slice(r, r + ROW_CHUNK)
            xn_ref[rows, :] = _rms(h_ref[rows, :], g_ref[...]).astype(BF16)

    z = _dot(xn_ref[...], w_ref[...].astype(BF16))

    def rope(x):
        lane = lax.broadcasted_iota(jnp.int32, (TMB, LANES), 1)
        first = (lane & (HEAD_DIM - 1)) < (ROT_DIM // 2)
        partner = jnp.where(first, pltpu.roll(x, LANES - ROT_DIM // 2, 1), pltpu.roll(x, ROT_DIM // 2, 1))
        return x * ca_ref[...] + partner * cb_ref[...]

    tiles = [slice(t * LANES, (t + 1) * LANES) for t in range(IN_BLOCK // LANES)]

    @pl.when(j < A_WIDTH // IN_BLOCK)
    def _():
        for sl in tiles:
            q_ref[:, sl] = (rope(z[:, sl]) * (HEAD_DIM ** -0.5)).astype(BF16)

    @pl.when(j == A_WIDTH // IN_BLOCK)
    def _():
        for sl in tiles[:KV_WIDTH // LANES]:
            k_ref[:, sl] = rope(z[:, sl])
        v_ref[...] = z[:, KV_WIDTH:]

    @pl.when((j > A_WIDTH // IN_BLOCK) & (j <= (A_WIDTH + B_WIDTH) // IN_BLOCK))
    def _():
        u_ref[...] = _gelu(z).astype(BF16)

    @pl.when(j > (A_WIDTH + B_WIDTH) // IN_BLOCK)
    def _():
        for sl in tiles:
            x = _gelu(z[:, sl])
            xc = x - jnp.mean(x, axis=-1, keepdims=True)
            y = xc * lax.rsqrt(jnp.mean(xc * xc, axis=-1, keepdims=True) + EPS)
            gv_ref[:, sl] = y * gv_g_ref[:, sl]


def _even_in(h, g, w, ca, cb, gv_g):
    nq = A_WIDTH // IN_BLOCK
    nu = nq + 1
    nv = nu + B_WIDTH // IN_BLOCK
    row = lambda i, j: (i, 0)
    const = lambda i, j: (0, 0)
    return pl.pallas_call(
        _even_in_kernel,
        grid=(N_TOK // TMB, EVEN_IN // IN_BLOCK),
        in_specs=[
            pl.BlockSpec((TMB, D_MODEL), row),
            pl.BlockSpec((1, D_MODEL), const),
            pl.BlockSpec((D_MODEL, IN_BLOCK), lambda i, j: (0, j)),
            pl.BlockSpec((TMB, LANES), row),
            pl.BlockSpec((TMB, LANES), row),
            pl.BlockSpec((1, IN_BLOCK), lambda i, j: (0, jnp.maximum(j - nv, 0))),
        ],
        out_specs=[
            pl.BlockSpec((TMB, IN_BLOCK), lambda i, j: (i, jnp.minimum(j, nq - 1))),
            pl.BlockSpec((TMB, KV_WIDTH), row),
            pl.BlockSpec((TMB, KV_WIDTH), row),
            pl.BlockSpec((TMB, IN_BLOCK), lambda i, j: (i, jnp.clip(j - nu, 0, nv - nu - 1))),
            pl.BlockSpec((TMB, IN_BLOCK), lambda i, j: (i, jnp.maximum(j - nv, 0))),
        ],
        out_shape=[
            jax.ShapeDtypeStruct((N_TOK, A_WIDTH), BF16),
            jax.ShapeDtypeStruct((N_TOK, KV_WIDTH), F32),
            jax.ShapeDtypeStruct((N_TOK, KV_WIDTH), F32),
            jax.ShapeDtypeStruct((N_TOK, B_WIDTH), BF16),
            jax.ShapeDtypeStruct((N_TOK, B_WIDTH), F32),
        ],
        scratch_shapes=[pltpu.VMEM((TMB, D_MODEL), BF16)],
        compiler_params=_params(("arbitrary", "arbitrary")),
        name="even_in",
    )(h, g, w, ca, cb, gv_g)


N_MIX_PROMPT = N_PROMPT // MIX_TM
ITEMS_PER_TILE = MIX_TM // DEC_SEQ
N_MIX = N_TOK // MIX_TM


def _spread_heads(x, kv_head):
    t = x[:, (kv_head // 2) * LANES:(kv_head // 2 + 1) * LANES]
    lane = lax.broadcasted_iota(jnp.int32, t.shape, 1)
    if kv_head % 2 == 0:
        lo = jnp.where(lane < HEAD_DIM, t, 0.0)
        hi = pltpu.roll(lo, HEAD_DIM, 1)
    else:
        hi = jnp.where(lane >= HEAD_DIM, t, 0.0)
        lo = pltpu.roll(hi, HEAD_DIM, 1)
    return lo, hi


def _attend(q2, kk, vv, sink_a, sink_b, allowed):
    s_len = kk[0].shape[0]
    kx = jnp.concatenate(kk, axis=0).astype(BF16)
    vx = jnp.concatenate(vv, axis=0).astype(BF16)
    s = lax.dot_general(q2, kx, _NT, preferred_element_type=F32)
    ps, inv = [], []
    for half, sink in ((0, sink_a), (1, sink_b)):
        sh = s[:, half * s_len:(half + 1) * s_len]
        if allowed is not None:
            sh = jnp.where(allowed, sh, -jnp.inf)
        m = jnp.maximum(jnp.max(sh, axis=-1, keepdims=True), sink)
        p = jnp.exp(sh - m)
        inv.append(1.0 / (jnp.sum(p, axis=-1, keepdims=True) + jnp.exp(sink - m)))
        ps.append(p)
    o = _dot(jnp.concatenate(ps, axis=1).astype(BF16), vx)
    lane = lax.broadcasted_iota(jnp.int32, o.shape, 1)
    return o * jnp.where(lane < HEAD_DIM, inv[0], inv[1])


def _mixer_kernel(sink_ref, q_ref, kp_ref, kc_ref, vp_ref, vc_ref, ck_ref, cv_ref,
                  u_ref, gv_ref, wp_ref, ws_ref, bp_ref, bs_ref,
                  ab_ref, ko_ref, vo_ref):
    i = pl.program_id(0)

    def gmlp(w_ref, b_ref):
        for g in range(B_GROUPS):
            sl = slice(g * GMLP_DIM, (g + 1) * GMLP_DIM)
            mixed = _dot(w_ref[g], gv_ref[:, sl].astype(BF16)) + b_ref[:, g:g + 1]
            ab_ref[:, A_WIDTH + g * GMLP_DIM:A_WIDTH + (g + 1) * GMLP_DIM] = (
                u_ref[:, sl].astype(F32) * mixed).astype(BF16)

    @pl.when(i < N_MIX_PROMPT)
    def _prompt():
        kk = jnp.concatenate([kp_ref[...], kc_ref[...]], axis=0)
        vv = jnp.concatenate([vp_ref[...], vc_ref[...]], axis=0)
        row = lax.broadcasted_iota(jnp.int32, (MIX_TM, 2 * MIX_TM), 0)
        col = lax.broadcasted_iota(jnp.int32, (MIX_TM, 2 * MIX_TM), 1)
        allowed = ((row < CHUNK) & (col < 3 * CHUNK)) | ((row >= CHUNK) & (col >= CHUNK))
        allowed = allowed & ((col >= MIX_TM) | (i > 0))
        for j in range(A_KV_HEADS):
            ks = _spread_heads(kk, j)
            vs = _spread_heads(vv, j)
            for p in (2 * j, 2 * j + 1):
                sl = slice(p * LANES, (p + 1) * LANES)
                o = _attend(q_ref[:, sl], ks, vs, sink_ref[2 * p], sink_ref[2 * p + 1], allowed)
                ab_ref[:, sl] = o.astype(BF16)
        gmlp(wp_ref, bp_ref)

    @pl.when(i >= N_MIX_PROMPT)
    def _sample():
        for a in range(ITEMS_PER_TILE):
            rows = slice(a * DEC_SEQ, (a + 1) * DEC_SEQ)
            k_new = kc_ref[rows, :]
            v_new = vc_ref[rows, :]
            kk = jnp.concatenate([ck_ref[a], k_new], axis=0)
            vv = jnp.concatenate([cv_ref[a], v_new], axis=0)
            ko_ref[a] = kk[DEC_SEQ:, :]
            vo_ref[a] = vv[DEC_SEQ:, :]
            for j in range(A_KV_HEADS):
                ks = _spread_heads(kk, j)
                vs = _spread_heads(vv, j)
                for p in (2 * j, 2 * j + 1):
                    sl = slice(p * LANES, (p + 1) * LANES)
                    o = _attend(q_ref[rows, sl], ks, vs, sink_ref[2 * p], sink_ref[2 * p + 1], None)
                    ab_ref[rows, sl] = o.astype(BF16)
        gmlp(ws_ref, bs_ref)


def _mixer(sinks, q, k, v, cache_k, cache_v, u, gv, w_p, w_s, b_p, b_s):
    row = lambda i: (i, 0)
    prev = lambda i: (jnp.maximum(i - 1, 0), 0)
    item = lambda i: (jnp.maximum(i - N_MIX_PROMPT, 0), 0, 0)
    c2 = lambda i: (0, 0)
    c3 = lambda i: (0, 0, 0)
    cache_block = (ITEMS_PER_TILE, WINDOW, KV_WIDTH)
    return pl.pallas_call(
        _mixer_kernel,
        grid=(N_MIX,),
        in_specs=[
            pl.BlockSpec(memory_space=pltpu.SMEM),
            pl.BlockSpec((MIX_TM, A_WIDTH), row),
            pl.BlockSpec((MIX_TM, KV_WIDTH), prev),
            pl.BlockSpec((MIX_TM, KV_WIDTH), row),
            pl.BlockSpec((MIX_TM, KV_WIDTH), prev),
            pl.BlockSpec((MIX_TM, KV_WIDTH), row),
            pl.BlockSpec(cache_block, item),
            pl.BlockSpec(cache_block, item),
            pl.BlockSpec((MIX_TM, B_WIDTH), row),
            pl.BlockSpec((MIX_TM, B_WIDTH), row),
            pl.BlockSpec((B_GROUPS, MIX_TM, MIX_TM), c3),
            pl.BlockSpec((B_GROUPS, MIX_TM, MIX_TM), c3),
            pl.BlockSpec((MIX_TM, B_GROUPS), c2),
            pl.BlockSpec((MIX_TM, B_GROUPS), c2),
        ],
        out_specs=[
            pl.BlockSpec((MIX_TM, A_WIDTH + B_WIDTH), row),
            pl.BlockSpec(cache_block, item),
            pl.BlockSpec(cache_block, item),
        ],
        out_shape=[
            jax.ShapeDtypeStruct((N_TOK, A_WIDTH + B_WIDTH), BF16),
            jax.ShapeDtypeStruct((DEC_BATCH, WINDOW, KV_WIDTH), F32),
            jax.ShapeDtypeStruct((DEC_BATCH, WINDOW, KV_WIDTH), F32),
        ],
        compiler_params=_params(("arbitrary",)),
        name="mixer",
    )(sinks, q, k, k, v, v, cache_k, cache_v, u, gv, w_p, w_s, b_p, b_s)


N_PJ = D_MODEL // PJ_BLOCK
N_FF = D_FF // FF_BLOCK


def _proj_ffn_kernel(h_ref, y_ref, wo_ref, gf_ref, wu_ref, wd_ref, gn_ref, *refs, final):
    if final:
        (o_ref, xn_ref) = refs
    else:
        (o_ref, nxt_ref, xn_ref) = refs
    s = pl.program_id(1)
    cols = [slice(c * MXU_N, (c + 1) * MXU_N) for c in range(D_MODEL // MXU_N)]
    row_chunks = [slice(r, r + ROW_CHUNK) for r in range(0, TMB, ROW_CHUNK)]

    def accumulate(lhs, w):
        for sl in cols:
            o_ref[:, sl] += _dot(lhs, w[:, sl])

    @pl.when(s == 0)
    def _():
        o_ref[...] = jnp.zeros_like(o_ref)

    @pl.when(s < N_PJ)
    def _():
        accumulate(y_ref[...], wo_ref[...].astype(BF16))

    for c in range(N_PJ):
        @pl.when(s == c)
        def _():
            sl = slice(c * PJ_BLOCK, (c + 1) * PJ_BLOCK)
            o_ref[:, sl] += h_ref[...]

    @pl.when(s == N_PJ)
    def _():
        for rows in row_chunks:
            xn_ref[rows, :] = _rms(o_ref[rows, :], gf_ref[...]).astype(BF16)

    @pl.when(s >= N_PJ)
    def _():
        t = jnp.maximum(_dot(xn_ref[...], wu_ref[...].astype(BF16)), 0.0)
        accumulate((t * t).astype(BF16), wd_ref[...].astype(BF16))

    @pl.when(s == N_PJ + N_FF - 1)
    def _():
        for rows in row_chunks:
            nrm = _rms(o_ref[rows, :], gn_ref[...])
            if final:
                o_ref[rows, :] = nrm
            else:
                nxt_ref[rows, :] = nrm.astype(BF16)


def _proj_ffn(h, y, w_out, g_ffn, w_up, w_down, layer, g_next, final):
    row = lambda i, s: (i, 0)
    const = lambda i, s: (0, 0)
    pj = lambda s: jnp.minimum(s, N_PJ - 1)
    ff = lambda s: jnp.maximum(s - N_PJ, 0)
    out_specs = [pl.BlockSpec((TMB, D_MODEL), row)]
    out_shape = [jax.ShapeDtypeStruct((N_TOK, D_MODEL), F32)]
    if not final:
        out_specs.append(pl.BlockSpec((TMB, D_MODEL), row))
        out_shape.append(jax.ShapeDtypeStruct((N_TOK, D_MODEL), BF16))
    return pl.pallas_call(
        functools.partial(_proj_ffn_kernel, final=final),
        grid=(N_TOK // TMB, N_PJ + N_FF),
        in_specs=[
            pl.BlockSpec((TMB, PJ_BLOCK), lambda i, s: (i, pj(s))),
            pl.BlockSpec((TMB, PJ_BLOCK), lambda i, s: (i, pj(s))),
            pl.BlockSpec((PJ_BLOCK, D_MODEL), lambda i, s: (pj(s), 0)),
            pl.BlockSpec((1, D_MODEL), const),
            pl.BlockSpec((None, D_MODEL, FF_BLOCK), lambda i, s: (layer, 0, ff(s))),
            pl.BlockSpec((None, FF_BLOCK, D_MODEL), lambda i, s: (layer, ff(s), 0)),
            pl.BlockSpec((1, D_MODEL), const),
        ],
        out_specs=out_specs,
        out_shape=out_shape,
        scratch_shapes=[pltpu.VMEM((TMB, D_MODEL), BF16)],
        compiler_params=_params(("arbitrary", "arbitrary")),
        name="proj_ffn_final" if final else "proj_ffn",
    )(h, y, w_out, g_ffn, w_up, w_down, g_next)


N_HG_PROMPT = N_PROMPT // TM
N_HG = N_TOK // TM
CHUNKS_PER_TILE = TM // HG_CHUNK
ITEMS_PER_CHUNK = HG_CHUNK // DEC_SEQ


def _hgrn_scores(q, k, f, b_ref, r0, seg):
    n = HG_CHUNK
    rows = lax.broadcasted_iota(jnp.int32, (n, n), 0)
    cols = lax.broadcasted_iota(jnp.int32, (n, n), 1)
    apart = rows ^ cols
    scores = None
    h = seg // 2
    while h >= 1:
        if h >= 8:
            qs, ks = [], []
            zero = jnp.zeros((h, C_DIM), F32)
            groups = range(n // (2 * h))
            for g in groups:
                base = r0 + g * 2 * h
                beta = b_ref[base + h - 1:base + h, :]
                lo = slice(g * 2 * h, g * 2 * h + h)
                up = slice(g * 2 * h + h, (g + 1) * 2 * h)
                ks += [k[lo] * jnp.exp(beta - b_ref[base:base + h, :]), zero]
                qs.append(q[up] * jnp.exp(b_ref[base + h:base + 2 * h, :] - beta))
            s_up = lax.dot_general(jnp.concatenate(qs, axis=0).astype(BF16),
                                   jnp.concatenate(ks, axis=0).astype(BF16), _NT,
                                   preferred_element_type=F32)
            pieces = []
            for g in groups:
                pieces += [jnp.zeros((h, n), F32), s_up[g * h:(g + 1) * h]]
            s_h = jnp.concatenate(pieces, axis=0)
        else:
            if h == 4:
                qs, ks = [], []
                upper = lax.broadcasted_iota(jnp.int32, (8, C_DIM), 0) >= 4
                for g in range(n // 8):
                    sl = slice(8 * g, 8 * g + 8)
                    base = r0 + 8 * g
                    e = jnp.exp(-jnp.abs(b_ref[base:base + 8, :] - b_ref[base + 3:base + 4, :]))
                    qs.append(jnp.where(upper, q[sl] * e, 0.0))
                    ks.append(jnp.where(upper, 0.0, k[sl] * e))
                qt = jnp.concatenate(qs, axis=0)
                kt = jnp.concatenate(ks, axis=0)
            elif h == 2:
                r4 = lax.broadcasted_iota(jnp.int32, (n, C_DIM), 0) & 3
                f_next = pltpu.roll(f, n - 1, 0)
                f_prev = pltpu.roll(f, 1, 0)
                kt = jnp.where(r4 == 0, k * f_next, jnp.where(r4 == 1, k, 0.0))
                qt = jnp.where(r4 == 2, q * f, jnp.where(r4 == 3, q * f * f_prev, 0.0))
            else:
                odd = (lax.broadcasted_iota(jnp.int32, (n, C_DIM), 0) & 1) == 1
                kt = jnp.where(odd, 0.0, k)
                qt = jnp.where(odd, q * f, 0.0)
            s_h = lax.dot_general(qt.astype(BF16), kt.astype(BF16), _NT, preferred_element_type=F32)
        if scores is None:
            scores = s_h if 2 * h == n else jnp.where(apart < 2 * h, s_h, 0.0)
        else:
            scores = jnp.where(apart < 2 * h, s_h, scores)
        h //= 2
    return jnp.where(apart == 0, jnp.sum(q * k, axis=1, keepdims=True), scores)


def _hgrn_kernel(xn0_ref, xn_next_ref, wq_ref, wf_ref, wi_ref, wg_ref, lb_ref, go_ref, trip_ref, tris_ref,
                 s0_ref, og_ref, sp_ref, ss_ref, w_ref, z_ref, f_ref, b_ref, st_ref):
    t = pl.program_id(1)

    lbx = lb_ref[...]
    e = jnp.exp(lbx - jnp.max(lbx, axis=0, keepdims=True))
    p = e / jnp.sum(e, axis=0, keepdims=True)
    lb = (p[0:1, :] + p[1:2, :]) - p[0:1, :]

    @pl.when(t == 0)
    def _():
        st_ref[...] = jnp.zeros_like(st_ref)
        for c, src in enumerate((wq_ref, wf_ref, wi_ref, wg_ref)):
            w_ref[:, c * C_DIM:(c + 1) * C_DIM] = src[...].astype(BF16)
        z_ref[...] = _dot(xn0_ref[...], w_ref[...])

    def cumulative_log_decay(tri_ref):
        lf = jnp.log(f_ref[...])
        lf = jnp.concatenate([lf[c * HG_CHUNK:(c + 1) * HG_CHUNK] for c in range(CHUNKS_PER_TILE)], axis=1)
        hi = lf.astype(BF16)
        lo = (lf - hi.astype(F32)).astype(BF16)
        bb = _dot(tri_ref[...], jnp.concatenate([hi, lo], axis=1))
        for c in range(CHUNKS_PER_TILE):
            lanes = slice(c * C_DIM, (c + 1) * C_DIM)
            b_ref[c * HG_CHUNK:(c + 1) * HG_CHUNK, :] = bb[:, lanes] + bb[:, TM:][:, lanes]

    def chunk(c, seg):
        r0 = c * HG_CHUNK
        rows = slice(r0, r0 + HG_CHUNK)
        zq = z_ref[rows, 0:C_DIM]
        f = f_ref[rows, :]
        v = z_ref[rows, 2 * C_DIM:3 * C_DIM]
        gate = _sigmoid(z_ref[rows, 3 * C_DIM:4 * C_DIM])
        q = zq * _sigmoid(zq)
        k = 1.0 - f
        scores = _hgrn_scores(q, k, f, b_ref, r0, seg)
        o = _dot(scores.astype(BF16), v.astype(BF16))
        b = b_ref[rows, :]
        qe = (q * jnp.exp(b)).astype(BF16)
        if seg == HG_CHUNK:
            s_t = st_ref[...]
            o = o + lax.dot_general(qe, s_t.astype(BF16), _NT, preferred_element_type=F32)
            b_last = b_ref[r0 + HG_CHUNK - 1:r0 + HG_CHUNK, :]
            ke = (k * jnp.exp(b_last - b)).astype(BF16)
            st_ref[...] = s_t * jnp.exp(b_last) + _dot(v.T.astype(BF16), ke)
        else:
            outs = []
            v_t = v.T.astype(BF16)
            row = lax.broadcasted_iota(jnp.int32, (HG_CHUNK, C_DIM), 0)
            for a in range(HG_CHUNK // seg):
                item = c * (HG_CHUNK // seg) + a
                r = slice(a * seg, (a + 1) * seg)
                s_t = s0_ref[item].T
                outs.append(lax.dot_general(qe[r], s_t.astype(BF16), _NT, preferred_element_type=F32))
                b_last = b_ref[r0 + (a + 1) * seg - 1:r0 + (a + 1) * seg, :]
                own = (row >= a * seg) & (row < (a + 1) * seg)
                ke = jnp.where(own, k * jnp.exp(jnp.minimum(b_last - b, 0.0)), 0.0).astype(BF16)
                s_new = s_t * jnp.exp(b_last) + _dot(v_t, ke)
                ss_ref[item] = s_new.T
            o = o + jnp.concatenate(outs, axis=0)
        y = _rms(o, go_ref[...]) * gate
        og_ref[rows, :] = y.astype(BF16)

    def tile(tri_ref, seg, project_next):
        if project_next:
            z_next = _dot(xn_next_ref[...], w_ref[...])
        f_ref[...] = lb + (1.0 - lb) * _sigmoid(z_ref[:, C_DIM:2 * C_DIM])
        cumulative_log_decay(tri_ref)
        for c in range(CHUNKS_PER_TILE):
            chunk(c, seg)
        if project_next:
            z_ref[...] = z_next

    @pl.when(t < N_HG_PROMPT)
    def _():
        tile(trip_ref, HG_CHUNK, True)

    @pl.when(t == N_HG_PROMPT - 1)
    def _():
        sp_ref[...] = st_ref[...].T

    @pl.when(t == N_HG_PROMPT)
    def _():
        tile(tris_ref, DEC_SEQ, False)


def _hgrn(xn, w_in, lb, g_onorm, tri_p, tri_s, state0):
    c2 = lambda j, t: (0, 0)
    head_col = lambda j, t: (0, j)
    w_specs = [pl.BlockSpec((D_MODEL, C_DIM), functools.partial(lambda j, t, c: (0, c * C_HEADS + j), c=c))
               for c in range(4)]
    return pl.pallas_call(
        _hgrn_kernel,
        grid=(C_HEADS, N_HG),
        in_specs=[
            pl.BlockSpec((TM, D_MODEL), c2),
            pl.BlockSpec((TM, D_MODEL), lambda j, t: (jnp.minimum(t + 1, N_HG - 1), 0)),
            *w_specs,
            pl.BlockSpec((2, C_DIM), head_col),
            pl.BlockSpec((1, C_DIM), head_col),
            pl.BlockSpec((HG_CHUNK, HG_CHUNK), c2),
            pl.BlockSpec((HG_CHUNK, HG_CHUNK), c2),
            pl.BlockSpec((DEC_BATCH, None, C_DIM, C_DIM), lambda j, t: (0, j, 0, 0)),
        ],
        out_specs=[
            pl.BlockSpec((TM, C_DIM), lambda j, t: (t, j)),
            pl.BlockSpec((None, C_DIM, C_DIM), lambda j, t: (j, 0, 0)),
            pl.BlockSpec((DEC_BATCH, None, C_DIM, C_DIM), lambda j, t: (0, j, 0, 0)),
        ],
        out_shape=[
            jax.ShapeDtypeStruct((N_TOK, C_HEADS * C_DIM), BF16),
            jax.ShapeDtypeStruct((C_HEADS, C_DIM, C_DIM), F32),
            jax.ShapeDtypeStruct((DEC_BATCH, C_HEADS, C_DIM, C_DIM), F32),
        ],
        scratch_shapes=[
            pltpu.VMEM((D_MODEL, 4 * C_DIM), BF16),
            pltpu.VMEM((TM, 4 * C_DIM), F32),
            pltpu.VMEM((TM, C_DIM), F32),
            pltpu.VMEM((TM, C_DIM), F32),
            pltpu.VMEM((C_DIM, C_DIM), F32),
        ],
        compiler_params=_params(("arbitrary", "arbitrary")),
        name="hgrn",
    )(xn, xn, w_in, w_in, w_in, w_in, lb, g_onorm, tri_p, tri_s, state0)


def _rope_tables():
    half = ROT_DIM // 2
    inv = ROPE_THETA ** (-(jnp.arange(half, dtype=F32) * (2.0 / ROT_DIM)))
    pos = jnp.concatenate([jnp.arange(N_PROMPT, dtype=jnp.int32),
                           jnp.tile(PAST_LEN + jnp.arange(DEC_SEQ, dtype=jnp.int32), DEC_BATCH)])
    ang = pos.astype(F32)[:, None] * inv[None, :]
    cos, sin = jnp.cos(ang), jnp.sin(ang)
    ones = jnp.ones((N_TOK, HEAD_DIM - ROT_DIM), F32)
    ca = jnp.concatenate([cos, cos, ones], axis=1)
    cb = jnp.concatenate([-sin, sin, 0.0 * ones], axis=1)
    return jnp.tile(ca, (1, LANES // HEAD_DIM)), jnp.tile(cb, (1, LANES // HEAD_DIM))


def kernel(x_prompt, x_sample, cache_swa_k, cache_swa_v, state_hgrn, norm_mix, norm_ffn, norm_final,
           w_in_even, w_out_even, attn_sinks, gmlp_vnorm, gmlp_ws, gmlp_bias, w_in_odd, w_out_odd,
           hgrn_lb, hgrn_onorm, w_ffn_up, w_ffn_down):
    h0 = jnp.concatenate([x_prompt.reshape(N_PROMPT, D_MODEL), x_sample.reshape(N_SAMPLE, D_MODEL)], axis=0)
    ca, cb = _rope_tables()

    q, k, v, u, gv = _even_in(h0, norm_mix[0:1], w_in_even[0], ca, cb, gmlp_vnorm[0:1])
    ws = gmlp_ws[0]
    cidx = jnp.arange(MIX_TM) // CHUNK
    w_p = jnp.where(cidx[:, None] >= cidx[None, :], ws, 0.0).astype(BF16)
    eye = jnp.eye(ITEMS_PER_TILE, dtype=F32)
    w_s = jnp.einsum('ab,gpq->gapbq', eye, ws[:, :DEC_SEQ, :DEC_SEQ]).reshape(B_GROUPS, MIX_TM, MIX_TM).astype(BF16)
    b_p = gmlp_bias[0].T
    b_s = jnp.tile(gmlp_bias[0][:, :DEC_SEQ], (1, ITEMS_PER_TILE)).T
    ab, k_s, v_s = _mixer(attn_sinks[0], q, k, v,
                          cache_swa_k[0].reshape(DEC_BATCH, WINDOW, KV_WIDTH),
                          cache_swa_v[0].reshape(DEC_BATCH, WINDOW, KV_WIDTH),
                          u, gv, w_p, w_s, b_p, b_s)
    h1, xn1 = _proj_ffn(h0, ab, w_out_even[0], norm_ffn[0:1], w_ffn_up, w_ffn_down, 0,
                        norm_mix[1:2], final=False)

    tri_p = jnp.tril(jnp.ones((HG_CHUNK, HG_CHUNK), F32))
    seg = jnp.arange(HG_CHUNK) // DEC_SEQ
    tri_s = jnp.where(seg[:, None] == seg[None, :], tri_p, 0.0)
    og, s_p, s_s = _hgrn(xn1, w_in_odd[0], hgrn_lb, hgrn_onorm[0:1], tri_p.astype(BF16), tri_s.astype(BF16),
                         state_hgrn[0])
    y, = _proj_ffn(h1, og, w_out_odd[0], norm_ffn[1:2], w_ffn_up, w_ffn_down, 1,
                   norm_final[None, :], final=True)

    kv_shape = (1, 1, WINDOW, A_KV_HEADS, HEAD_DIM)
    kv_s_shape = (1, DEC_BATCH, WINDOW, A_KV_HEADS, HEAD_DIM)
    return (y[:N_PROMPT].reshape(1, N_PROMPT, D_MODEL),
            y[N_PROMPT:].reshape(DEC_BATCH, DEC_SEQ, D_MODEL),
            k[N_PROMPT - WINDOW:N_PROMPT].reshape(kv_shape),
            v[N_PROMPT - WINDOW:N_PROMPT].reshape(kv_shape),
            k_s.reshape(kv_s_shape),
            v_s.reshape(kv_s_shape),
            gv[N_PROMPT:].reshape(1, DEC_BATCH, DEC_SEQ, B_GROUPS, GMLP_DIM),
            s_p[None, None],
            s_s[None])
```

```python
import functools
import math

import jax
import jax.numpy as jnp
from jax import lax
from jax.experimental import pallas as pl
from jax.experimental.pallas import tpu as pltpu

F32 = jnp.float32
BF16 = jnp.bfloat16

D_MODEL = 2048
N_PROMPT = 8192
DEC_BATCH = 16
DEC_SEQ = 32
N_SAMPLE = DEC_BATCH * DEC_SEQ
N_TOK = N_PROMPT + N_SAMPLE
PAST_LEN = 2048
EPS = 1e-6

HEAD_DIM = 64
A_HEADS = 16
A_KV_HEADS = 4
A_WIDTH = A_HEADS * HEAD_DIM
KV_WIDTH = A_KV_HEADS * HEAD_DIM
ROT_DIM = 16
ROPE_THETA = 500000.0
WINDOW = 128
CHUNK = 64
B_GROUPS = 8
GMLP_DIM = 128
B_WIDTH = B_GROUPS * GMLP_DIM
EVEN_IN = A_WIDTH + 2 * KV_WIDTH + 2 * B_WIDTH
C_HEADS = 16
C_DIM = 128
D_FF = 4 * D_MODEL

LANES = 128
TM = 512
TMB = 1088
ROW_CHUNK = 272
N_TILES = N_TOK // TMB
TAIL = N_PROMPT - (N_TILES - 1) * TMB
MIX_TM = 128
HG_CHUNK = 128
IN_BLOCK = 512
PJ_BLOCK = 256
FF_BLOCK = 512
MXU_N = 256
VMEM_LIMIT = 58 * 1024 * 1024

_NT = (((1,), (1,)), ((), ()))


def _params(sem):
    return pltpu.CompilerParams(dimension_semantics=sem, vmem_limit_bytes=VMEM_LIMIT)


def _rms(x, g):
    return x * lax.rsqrt(jnp.mean(x * x, axis=-1, keepdims=True) + EPS) * g


def _sigmoid(x):
    return 0.5 * jnp.tanh(0.5 * x) + 0.5


def _gelu(x):
    c = math.sqrt(2.0 / math.pi)
    return x * (0.5 * (1.0 + jnp.tanh(c * (x + 0.044715 * (x * x * x)))))


def _dot(a, b):
    return jnp.dot(a, b, preferred_element_type=F32)


def _row_spans(start, stop, step=ROW_CHUNK):
    return [(r, min(r + step, stop)) for r in range(start, stop, step)]


def _even_in_kernel(xp_ref, xs_ref, g_ref, w_ref, ca_ref, cb_ref, gv_g_ref,
                    q_ref, k_ref, v_ref, u_ref, gv_ref, xn_ref):
    i = pl.program_id(0)
    j = pl.program_id(1)

    def norm_rows(src_ref, src0, dst0, n):
        for a, b in _row_spans(0, n):
            xn_ref[dst0 + a:dst0 + b, :] = _rms(src_ref[src0 + a:src0 + b, :], g_ref[...]).astype(BF16)

    @pl.when((j == 0) & (i < N_TILES - 1))
    def _():
        norm_rows(xp_ref, 0, 0, TMB)

    @pl.when((j == 0) & (i == N_TILES - 1))
    def _():
        norm_rows(xp_ref, 0, 0, TAIL)
        norm_rows(xs_ref, 0, TAIL, N_SAMPLE)

    z = _dot(xn_ref[...], w_ref[...].astype(BF16))

    def rope(x):
        lane = lax.broadcasted_iota(jnp.int32, (TMB, LANES), 1)
        first = (lane & (HEAD_DIM - 1)) < (ROT_DIM // 2)
        partner = jnp.where(first, pltpu.roll(x, LANES - ROT_DIM // 2, 1), pltpu.roll(x, ROT_DIM // 2, 1))
        return x * ca_ref[...] + partner * cb_ref[...]

    tiles = [slice(t * LANES, (t + 1) * LANES) for t in range(IN_BLOCK // LANES)]

    @pl.when(j < A_WIDTH // IN_BLOCK)
    def _():
        for sl in tiles:
            q_ref[:, sl] = (rope(z[:, sl]) * (HEAD_DIM ** -0.5)).astype(BF16)

    @pl.when(j == A_WIDTH // IN_BLOCK)
    def _():
        for sl in tiles[:KV_WIDTH // LANES]:
            k_ref[:, sl] = rope(z[:, sl])
        v_ref[...] = z[:, KV_WIDTH:]

    @pl.when((j > A_WIDTH // IN_BLOCK) & (j <= (A_WIDTH + B_WIDTH) // IN_BLOCK))
    def _():
        u_ref[...] = _gelu(z).astype(BF16)

    @pl.when(j > (A_WIDTH + B_WIDTH) // IN_BLOCK)
    def _():
        for sl in tiles:
            x = _gelu(z[:, sl])
            xc = x - jnp.mean(x, axis=-1, keepdims=True)
            y = xc * lax.rsqrt(jnp.mean(xc * xc, axis=-1, keepdims=True) + EPS)
            gv_ref[:, sl] = y * gv_g_ref[:, sl]


def _even_in(x_prompt, x_sample, g, w, ca, cb, gv_g):
    nq = A_WIDTH // IN_BLOCK
    nu = nq + 1
    nv = nu + B_WIDTH // IN_BLOCK
    row = lambda i, j: (i, 0)
    const = lambda i, j: (0, 0)
    return pl.pallas_call(
        _even_in_kernel,
        grid=(N_TILES, EVEN_IN // IN_BLOCK),
        in_specs=[
            pl.BlockSpec((TMB, D_MODEL), row),
            pl.BlockSpec((N_SAMPLE, D_MODEL), const),
            pl.BlockSpec((1, D_MODEL), const),
            pl.BlockSpec((D_MODEL, IN_BLOCK), lambda i, j: (0, j)),
            pl.BlockSpec((TMB, LANES), row),
            pl.BlockSpec((TMB, LANES), row),
            pl.BlockSpec((1, IN_BLOCK), lambda i, j: (0, jnp.maximum(j - nv, 0))),
        ],
        out_specs=[
            pl.BlockSpec((TMB, IN_BLOCK), lambda i, j: (i, jnp.minimum(j, nq - 1))),
            pl.BlockSpec((TMB, KV_WIDTH), row),
            pl.BlockSpec((TMB, KV_WIDTH), row),
            pl.BlockSpec((TMB, IN_BLOCK), lambda i, j: (i, jnp.clip(j - nu, 0, nv - nu - 1))),
            pl.BlockSpec((TMB, IN_BLOCK), lambda i, j: (i, jnp.maximum(j - nv, 0))),
        ],
        out_shape=[
            jax.ShapeDtypeStruct((N_TOK, A_WIDTH), BF16),
            jax.ShapeDtypeStruct((N_TOK, KV_WIDTH), F32),
            jax.ShapeDtypeStruct((N_TOK, KV_WIDTH), F32),
            jax.ShapeDtypeStruct((N_TOK, B_WIDTH), BF16),
            jax.ShapeDtypeStruct((N_TOK, B_WIDTH), F32),
        ],
        scratch_shapes=[pltpu.VMEM((TMB, D_MODEL), BF16)],
        compiler_params=_params(("arbitrary", "arbitrary")),
        name="even_in",
    )(x_prompt, x_sample, g, w, ca, cb, gv_g)


N_MIX_PROMPT = N_PROMPT // MIX_TM
ITEMS_PER_TILE = MIX_TM // DEC_SEQ
N_MIX = N_TOK // MIX_TM


def _spread_heads(x, kv_head):
    t = x[:, (kv_head // 2) * LANES:(kv_head // 2 + 1) * LANES]
    lane = lax.broadcasted_iota(jnp.int32, t.shape, 1)
    if kv_head % 2 == 0:
        lo = jnp.where(lane < HEAD_DIM, t, 0.0)
        hi = pltpu.roll(lo, HEAD_DIM, 1)
    else:
        hi = jnp.where(lane >= HEAD_DIM, t, 0.0)
        lo = pltpu.roll(hi, HEAD_DIM, 1)
    return lo, hi


def _attend(q2, kk, vv, sink_a, sink_b, allowed):
    s_len = kk[0].shape[0]
    kx = jnp.concatenate(kk, axis=0).astype(BF16)
    vx = jnp.concatenate(vv, axis=0).astype(BF16)
    s = lax.dot_general(q2, kx, _NT, preferred_element_type=F32)
    ps, inv = [], []
    for half, sink in ((0, sink_a), (1, sink_b)):
        sh = s[:, half * s_len:(half + 1) * s_len]
        if allowed is not None:
            sh = jnp.where(allowed, sh, -jnp.inf)
        m = jnp.maximum(jnp.max(sh, axis=-1, keepdims=True), sink)
        p = jnp.exp(sh - m)
        inv.append(1.0 / (jnp.sum(p, axis=-1, keepdims=True) + jnp.exp(sink - m)))
        ps.append(p)
    o = _dot(jnp.concatenate(ps, axis=1).astype(BF16), vx)
    lane = lax.broadcasted_iota(jnp.int32, o.shape, 1)
    return o * jnp.where(lane < HEAD_DIM, inv[0], inv[1])


def _mixer_kernel(sink_ref, q_ref, kp_ref, kc_ref, vp_ref, vc_ref, ck_ref, cv_ref,
                  u_ref, gv_ref, wp_ref, ws_ref, bp_ref, bs_ref,
                  ab_ref, ko_ref, vo_ref):
    i = pl.program_id(0)

    def gmlp(w_ref, b_ref):
        for g in range(B_GROUPS):
            sl = slice(g * GMLP_DIM, (g + 1) * GMLP_DIM)
            mixed = _dot(w_ref[g], gv_ref[:, sl].astype(BF16)) + b_ref[:, g:g + 1]
            ab_ref[:, A_WIDTH + g * GMLP_DIM:A_WIDTH + (g + 1) * GMLP_DIM] = (
                u_ref[:, sl].astype(F32) * mixed).astype(BF16)

    @pl.when(i < N_MIX_PROMPT)
    def _prompt():
        kk = jnp.concatenate([kp_ref[...], kc_ref[...]], axis=0)
        vv = jnp.concatenate([vp_ref[...], vc_ref[...]], axis=0)
        row = lax.broadcasted_iota(jnp.int32, (MIX_TM, 2 * MIX_TM), 0)
        col = lax.broadcasted_iota(jnp.int32, (MIX_TM, 2 * MIX_TM), 1)
        allowed = ((row < CHUNK) & (col < 3 * CHUNK)) | ((row >= CHUNK) & (col >= CHUNK))
        allowed = allowed & ((col >= MIX_TM) | (i > 0))
        for j in range(A_KV_HEADS):
            ks = _spread_heads(kk, j)
            vs = _spread_heads(vv, j)
            for p in (2 * j, 2 * j + 1):
                sl = slice(p * LANES, (p + 1) * LANES)
                o = _attend(q_ref[:, sl], ks, vs, sink_ref[2 * p], sink_ref[2 * p + 1], allowed)
                ab_ref[:, sl] = o.astype(BF16)
        gmlp(wp_ref, bp_ref)

    @pl.when(i >= N_MIX_PROMPT)
    def _sample():
        for a in range(ITEMS_PER_TILE):
            rows = slice(a * DEC_SEQ, (a + 1) * DEC_SEQ)
            k_new = kc_ref[rows, :]
            v_new = vc_ref[rows, :]
            kk = jnp.concatenate([ck_ref[a], k_new], axis=0)
            vv = jnp.concatenate([cv_ref[a], v_new], axis=0)
            ko_ref[a] = kk[DEC_SEQ:, :]
            vo_ref[a] = vv[DEC_SEQ:, :]
            for j in range(A_KV_HEADS):
                ks = _spread_heads(kk, j)
                vs = _spread_heads(vv, j)
                for p in (2 * j, 2 * j + 1):
                    sl = slice(p * LANES, (p + 1) * LANES)
                    o = _attend(q_ref[rows, sl], ks, vs, sink_ref[2 * p], sink_ref[2 * p + 1], None)
                    ab_ref[rows, sl] = o.astype(BF16)
        gmlp(ws_ref, bs_ref)


def _mixer(sinks, q, k, v, cache_k, cache_v, u, gv, w_p, w_s, b_p, b_s):
    row = lambda i: (i, 0)
    prev = lambda i: (jnp.maximum(i - 1, 0), 0)
    item = lambda i: (jnp.maximum(i - N_MIX_PROMPT, 0), 0, 0)
    c2 = lambda i: (0, 0)
    c3 = lambda i: (0, 0, 0)
    cache_block = (ITEMS_PER_TILE, WINDOW, KV_WIDTH)
    return pl.pallas_call(
        _mixer_kernel,
        grid=(N_MIX,),
        in_specs=[
            pl.BlockSpec(memory_space=pltpu.SMEM),
            pl.BlockSpec((MIX_TM, A_WIDTH), row),
            pl.BlockSpec((MIX_TM, KV_WIDTH), prev),
            pl.BlockSpec((MIX_TM, KV_WIDTH), row),
            pl.BlockSpec((MIX_TM, KV_WIDTH), prev),
            pl.BlockSpec((MIX_TM, KV_WIDTH), row),
            pl.BlockSpec(cache_block, item),
            pl.BlockSpec(cache_block, item),
            pl.BlockSpec((MIX_TM, B_WIDTH), row),
            pl.BlockSpec((MIX_TM, B_WIDTH), row),
            pl.BlockSpec((B_GROUPS, MIX_TM, MIX_TM), c3),
            pl.BlockSpec((B_GROUPS, MIX_TM, MIX_TM), c3),
            pl.BlockSpec((MIX_TM, B_GROUPS), c2),
            pl.BlockSpec((MIX_TM, B_GROUPS), c2),
        ],
        out_specs=[
            pl.BlockSpec((MIX_TM, A_WIDTH + B_WIDTH), row),
            pl.BlockSpec(cache_block, item),
            pl.BlockSpec(cache_block, item),
        ],
        out_shape=[
            jax.ShapeDtypeStruct((N_TOK, A_WIDTH + B_WIDTH), BF16),
            jax.ShapeDtypeStruct((DEC_BATCH, WINDOW, KV_WIDTH), F32),
            jax.ShapeDtypeStruct((DEC_BATCH, WINDOW, KV_WIDTH), F32),
        ],
        compiler_params=_params(("arbitrary",)),
        name="mixer",
    )(sinks, q, k, k, v, v, cache_k, cache_v, u, gv, w_p, w_s, b_p, b_s)


N_PJ = D_MODEL // PJ_BLOCK
N_FF = D_FF // FF_BLOCK


def _proj_ffn_kernel(*refs, split_in, final):
    refs = list(refs)
    h_refs = [refs.pop(0) for _ in range(2 if split_in else 1)]
    y_ref, wo_ref, gf_ref, wu_ref, wd_ref, gn_ref, o_ref = refs[:7]
    if final:
        ys_ref, xn_ref = refs[7:]
    else:
        (xn_ref,) = refs[7:]
    i = pl.program_id(0)
    s = pl.program_id(1)
    last_tile = i == N_TILES - 1
    cols = [slice(c * MXU_N, (c + 1) * MXU_N) for c in range(D_MODEL // MXU_N)]

    def accumulate(lhs, w):
        for sl in cols:
            o_ref[:, sl] += _dot(lhs, w[:, sl])

    @pl.when(s == 0)
    def _():
        o_ref[...] = jnp.zeros_like(o_ref)

    @pl.when(s < N_PJ)
    def _():
        accumulate(y_ref[...], wo_ref[...].astype(BF16))

    for c in range(N_PJ):
        sl = slice(c * PJ_BLOCK, (c + 1) * PJ_BLOCK)
        if split_in:
            hp_ref, hs_ref = h_refs

            @pl.when((s == c) & jnp.logical_not(last_tile))
            def _():
                o_ref[:, sl] += hp_ref[...]

            @pl.when((s == c) & last_tile)
            def _():
                o_ref[0:TAIL, sl] += hp_ref[0:TAIL, :]
                o_ref[TAIL:, sl] += hs_ref[...]
        else:
            @pl.when(s == c)
            def _():
                o_ref[:, sl] += h_refs[0][...]

    @pl.when(s == N_PJ)
    def _():
        for a, b in _row_spans(0, TMB):
            xn_ref[a:b, :] = _rms(o_ref[a:b, :], gf_ref[...]).astype(BF16)

    @pl.when(s >= N_PJ)
    def _():
        t = jnp.maximum(_dot(xn_ref[...], wu_ref[...].astype(BF16)), 0.0)
        accumulate((t * t).astype(BF16), wd_ref[...].astype(BF16))

    last_step = s == N_PJ + N_FF - 1
    if final:
        @pl.when(last_step & jnp.logical_not(last_tile))
        def _():
            for a, b in _row_spans(0, TMB):
                o_ref[a:b, :] = _rms(o_ref[a:b, :], gn_ref[...])

        @pl.when(last_step & last_tile)
        def _():
            for a, b in _row_spans(0, TAIL):
                o_ref[a:b, :] = _rms(o_ref[a:b, :], gn_ref[...])
            for a, b in _row_spans(0, N_SAMPLE):
                ys_ref[a:b, :] = _rms(o_ref[TAIL + a:TAIL + b, :], gn_ref[...])
    else:
        @pl.when(last_step)
        def _():
            for a, b in _row_spans(0, TMB):
                xn_ref[a:b, :] = _rms(o_ref[a:b, :], gn_ref[...]).astype(BF16)


def _proj_ffn(h, y, w_out, g_ffn, w_up, w_down, layer, g_next, final):
    split_in = isinstance(h, tuple)
    row = lambda i, s: (i, 0)
    const = lambda i, s: (0, 0)
    pj = lambda s: jnp.minimum(s, N_PJ - 1)
    ff = lambda s: jnp.maximum(s - N_PJ, 0)
    h_specs = [pl.BlockSpec((TMB, PJ_BLOCK), lambda i, s: (i, pj(s)))]
    if split_in:
        h_specs.append(pl.BlockSpec((N_SAMPLE, PJ_BLOCK), lambda i, s: (0, pj(s))))
    if final:
        out_specs = [pl.BlockSpec((TMB, D_MODEL), row), pl.BlockSpec((N_SAMPLE, D_MODEL), const)]
        out_shape = [jax.ShapeDtypeStruct((N_PROMPT, D_MODEL), F32),
                     jax.ShapeDtypeStruct((N_SAMPLE, D_MODEL), F32)]
        scratch = [pltpu.VMEM((TMB, D_MODEL), BF16)]
    else:
        out_specs = [pl.BlockSpec((TMB, D_MODEL), row), pl.BlockSpec((TMB, D_MODEL), row)]
        out_shape = [jax.ShapeDtypeStruct((N_TOK, D_MODEL), F32),
                     jax.ShapeDtypeStruct((N_TOK, D_MODEL), BF16)]
        scratch = []
    return pl.pallas_call(
        functools.partial(_proj_ffn_kernel, split_in=split_in, final=final),
        grid=(N_TILES, N_PJ + N_FF),
        in_specs=[
            *h_specs,
            pl.BlockSpec((TMB, PJ_BLOCK), lambda i, s: (i, pj(s))),
            pl.BlockSpec((PJ_BLOCK, D_MODEL), lambda i, s: (pj(s), 0)),
            pl.BlockSpec((1, D_MODEL), const),
            pl.BlockSpec((None, D_MODEL, FF_BLOCK), lambda i, s: (layer, 0, ff(s))),
            pl.BlockSpec((None, FF_BLOCK, D_MODEL), lambda i, s: (layer, ff(s), 0)),
            pl.BlockSpec((1, D_MODEL), const),
        ],
        out_specs=out_specs,
        out_shape=out_shape,
        scratch_shapes=scratch,
        compiler_params=_params(("arbitrary", "arbitrary")),
        name="proj_ffn_final" if final else "proj_ffn",
    )(*(h if split_in else (h,)), y, w_out, g_ffn, w_up, w_down, g_next)


N_HG_PROMPT = N_PROMPT // TM
N_HG = N_TOK // TM
CHUNKS_PER_TILE = TM // HG_CHUNK
ITEMS_PER_CHUNK = HG_CHUNK // DEC_SEQ


def _hgrn_scores(q, k, f, b_ref, r0, seg):
    n = HG_CHUNK
    rows = lax.broadcasted_iota(jnp.int32, (n, n), 0)
    cols = lax.broadcasted_iota(jnp.int32, (n, n), 1)
    apart = rows ^ cols
    scores = None
    h = seg // 2
    while h >= 1:
        if h >= 8:
            qs, ks = [], []
            zero = jnp.zeros((h, C_DIM), F32)
            groups = range(n // (2 * h))
            for g in groups:
                base = r0 + g * 2 * h
                beta = b_ref[base + h - 1:base + h, :]
                lo = slice(g * 2 * h, g * 2 * h + h)
                up = slice(g * 2 * h + h, (g + 1) * 2 * h)
                ks += [k[lo] * jnp.exp(beta - b_ref[base:base + h, :]), zero]
                qs.append(q[up] * jnp.exp(b_ref[base + h:base + 2 * h, :] - beta))
            s_up = lax.dot_general(jnp.concatenate(qs, axis=0).astype(BF16),
                                   jnp.concatenate(ks, axis=0).astype(BF16), _NT,
                                   preferred_element_type=F32)
            pieces = []
            for g in groups:
                pieces += [jnp.zeros((h, n), F32), s_up[g * h:(g + 1) * h]]
            s_h = jnp.concatenate(pieces, axis=0)
        else:
            if h == 4:
                qs, ks = [], []
                upper = lax.broadcasted_iota(jnp.int32, (8, C_DIM), 0) >= 4
                for g in range(n // 8):
                    sl = slice(8 * g, 8 * g + 8)
                    base = r0 + 8 * g
                    e = jnp.exp(-jnp.abs(b_ref[base:base + 8, :] - b_ref[base + 3:base + 4, :]))
                    qs.append(jnp.where(upper, q[sl] * e, 0.0))
                    ks.append(jnp.where(upper, 0.0, k[sl] * e))
                qt = jnp.concatenate(qs, axis=0)
                kt = jnp.concatenate(ks, axis=0)
            elif h == 2:
                r4 = lax.broadcasted_iota(jnp.int32, (n, C_DIM), 0) & 3
                f_next = pltpu.roll(f, n - 1, 0)
                f_prev = pltpu.roll(f, 1, 0)
                kt = jnp.where(r4 == 0, k * f_next, jnp.where(r4 == 1, k, 0.0))
                qt = jnp.where(r4 == 2, q * f, jnp.where(r4 == 3, q * f * f_prev, 0.0))
            else:
                odd = (lax.broadcasted_iota(jnp.int32, (n, C_DIM), 0) & 1) == 1
                kt = jnp.where(odd, 0.0, k)
                qt = jnp.where(odd, q * f, 0.0)
            s_h = lax.dot_general(qt.astype(BF16), kt.astype(BF16), _NT, preferred_element_type=F32)
        if scores is None:
            scores = s_h if 2 * h == n else jnp.where(apart < 2 * h, s_h, 0.0)
        else:
            scores = jnp.where(apart < 2 * h, s_h, scores)
        h //= 2
    return jnp.where(apart == 0, jnp.sum(q * k, axis=1, keepdims=True), scores)


def _hgrn_kernel(xn0_ref, xn_next_ref, wq_ref, wf_ref, wi_ref, wg_ref, lb_ref, go_ref, trip_ref, tris_ref,
                 s0_ref, og_ref, sp_ref, ss_ref, w_ref, z_ref, f_ref, b_ref, st_ref):
    t = pl.program_id(1)

    lbx = lb_ref[...]
    e = jnp.exp(lbx - jnp.max(lbx, axis=0, keepdims=True))
    p = e / jnp.sum(e, axis=0, keepdims=True)
    lb = (p[0:1, :] + p[1:2, :]) - p[0:1, :]

    @pl.when(t == 0)
    def _():
        st_ref[...] = jnp.zeros_like(st_ref)
        for c, src in enumerate((wq_ref, wf_ref, wi_ref, wg_ref)):
            w_ref[:, c * C_DIM:(c + 1) * C_DIM] = src[...].astype(BF16)
        z_ref[...] = _dot(xn0_ref[...], w_ref[...])

    def cumulative_log_decay(tri_ref):
        lf = jnp.log(f_ref[...])
        lf = jnp.concatenate([lf[c * HG_CHUNK:(c + 1) * HG_CHUNK] for c in range(CHUNKS_PER_TILE)], axis=1)
        hi = lf.astype(BF16)
        lo = (lf - hi.astype(F32)).astype(BF16)
        bb = _dot(tri_ref[...], jnp.concatenate([hi, lo], axis=1))
        for c in range(CHUNKS_PER_TILE):
            lanes = slice(c * C_DIM, (c + 1) * C_DIM)
            b_ref[c * HG_CHUNK:(c + 1) * HG_CHUNK, :] = bb[:, lanes] + bb[:, TM:][:, lanes]

    def chunk(c, seg):
        r0 = c * HG_CHUNK
        rows = slice(r0, r0 + HG_CHUNK)
        zq = z_ref[rows, 0:C_DIM]
        f = f_ref[rows, :]
        v = z_ref[rows, 2 * C_DIM:3 * C_DIM]
        gate = _sigmoid(z_ref[rows, 3 * C_DIM:4 * C_DIM])
        q = zq * _sigmoid(zq)
        k = 1.0 - f
        scores = _hgrn_scores(q, k, f, b_ref, r0, seg)
        o = _dot(scores.astype(BF16), v.astype(BF16))
        b = b_ref[rows, :]
        qe = (q * jnp.exp(b)).astype(BF16)
        if seg == HG_CHUNK:
            s_t = st_ref[...]
            o = o + lax.dot_general(qe, s_t.astype(BF16), _NT, preferred_element_type=F32)
            b_last = b_ref[r0 + HG_CHUNK - 1:r0 + HG_CHUNK, :]
            ke = (k * jnp.exp(b_last - b)).astype(BF16)
            st_ref[...] = s_t * jnp.exp(b_last) + _dot(v.T.astype(BF16), ke)
        else:
            outs = []
            v_t = v.T.astype(BF16)
            row = lax.broadcasted_iota(jnp.int32, (HG_CHUNK, C_DIM), 0)
            for a in range(HG_CHUNK // seg):
                item = c * (HG_CHUNK // seg) + a
                r = slice(a * seg, (a + 1) * seg)
                s_t = s0_ref[item].T
                outs.append(lax.dot_general(qe[r], s_t.astype(BF16), _NT, preferred_element_type=F32))
                b_last = b_ref[r0 + (a + 1) * seg - 1:r0 + (a + 1) * seg, :]
                own = (row >= a * seg) & (row < (a + 1) * seg)
                ke = jnp.where(own, k * jnp.exp(jnp.minimum(b_last - b, 0.0)), 0.0).astype(BF16)
                s_new = s_t * jnp.exp(b_last) + _dot(v_t, ke)
                ss_ref[item] = s_new.T
            o = o + jnp.concatenate(outs, axis=0)
        y = _rms(o, go_ref[...]) * gate
        og_ref[rows, :] = y.astype(BF16)

    def tile(tri_ref, seg, project_next):
        f_ref[...] = lb + (1.0 - lb) * _sigmoid(z_ref[:, C_DIM:2 * C_DIM])
        cumulative_log_decay(tri_ref)
        z_next = []
        for c in range(CHUNKS_PER_TILE):
            chunk(c, seg)
            if project_next:
                z_next.append(_dot(xn_next_ref[c * HG_CHUNK:(c + 1) * HG_CHUNK, :], w_ref[...]))
        if project_next:
            z_ref[...] = jnp.concatenate(z_next, axis=0)

    @pl.when(t < N_HG_PROMPT)
    def _():
        tile(trip_ref, HG_CHUNK, True)

    @pl.when(t == N_HG_PROMPT - 1)
    def _():
        sp_ref[...] = st_ref[...].T

    @pl.when(t == N_HG_PROMPT)
    def _():
        tile(tris_ref, DEC_SEQ, False)


def _hgrn(xn, w_in, lb, g_onorm, tri_p, tri_s, state0):
    c2 = lambda j, t: (0, 0)
    head_col = lambda j, t: (0, j)
    w_specs = [pl.BlockSpec((D_MODEL, C_DIM), functools.partial(lambda j, t, c: (0, c * C_HEADS + j), c=c))
               for c in range(4)]
    return pl.pallas_call(
        _hgrn_kernel,
        grid=(C_HEADS, N_HG),
        in_specs=[
            pl.BlockSpec((TM, D_MODEL), c2),
            pl.BlockSpec((TM, D_MODEL), lambda j, t: (jnp.minimum(t + 1, N_HG - 1), 0)),
            *w_specs,
            pl.BlockSpec((2, C_DIM), head_col),
            pl.BlockSpec((1, C_DIM), head_col),
            pl.BlockSpec((HG_CHUNK, HG_CHUNK), c2),
            pl.BlockSpec((HG_CHUNK, HG_CHUNK), c2),
            pl.BlockSpec((DEC_BATCH, None, C_DIM, C_DIM), lambda j, t: (0, j, 0, 0)),
        ],
        out_specs=[
            pl.BlockSpec((TM, C_DIM), lambda j, t: (t, j)),
            pl.BlockSpec((None, C_DIM, C_DIM), lambda j, t: (j, 0, 0)),
            pl.BlockSpec((DEC_BATCH, None, C_DIM, C_DIM), lambda j, t: (0, j, 0, 0)),
        ],
        out_shape=[
            jax.ShapeDtypeStruct((N_TOK, C_HEADS * C_DIM), BF16),
            jax.ShapeDtypeStruct((C_HEADS, C_DIM, C_DIM), F32),
            jax.ShapeDtypeStruct((DEC_BATCH, C_HEADS, C_DIM, C_DIM), F32),
        ],
        scratch_shapes=[
            pltpu.VMEM((D_MODEL, 4 * C_DIM), BF16),
            pltpu.VMEM((TM, 4 * C_DIM), F32),
            pltpu.VMEM((TM, C_DIM), F32),
            pltpu.VMEM((TM, C_DIM), F32),
            pltpu.VMEM((C_DIM, C_DIM), F32),
        ],
        compiler_params=_params(("arbitrary", "arbitrary")),
        name="hgrn",
    )(xn, xn, w_in, w_in, w_in, w_in, lb, g_onorm, tri_p, tri_s, state0)


def _rope_tables():
    half = ROT_DIM // 2
    inv = ROPE_THETA ** (-(jnp.arange(half, dtype=F32) * (2.0 / ROT_DIM)))
    pos = jnp.concatenate([jnp.arange(N_PROMPT, dtype=jnp.int32),
                           jnp.tile(PAST_LEN + jnp.arange(DEC_SEQ, dtype=jnp.int32), DEC_BATCH)])
    ang = pos.astype(F32)[:, None] * inv[None, :]
    cos, sin = jnp.cos(ang), jnp.sin(ang)
    ones = jnp.ones((N_TOK, HEAD_DIM - ROT_DIM), F32)
    ca = jnp.concatenate([cos, cos, ones], axis=1)
    cb = jnp.concatenate([-sin, sin, 0.0 * ones], axis=1)
    return jnp.tile(ca, (1, LANES // HEAD_DIM)), jnp.tile(cb, (1, LANES // HEAD_DIM))


def kernel(x_prompt, x_sample, cache_swa_k, cache_swa_v, state_hgrn, norm_mix, norm_ffn, norm_final,
           w_in_even, w_out_even, attn_sinks, gmlp_vnorm, gmlp_ws, gmlp_bias, w_in_odd, w_out_odd,
           hgrn_lb, hgrn_onorm, w_ffn_up, w_ffn_down):
    xp = x_prompt.reshape(N_PROMPT, D_MODEL)
    xs = x_sample.reshape(N_SAMPLE, D_MODEL)
    ca, cb = _rope_tables()

    q, k, v, u, gv = _even_in(xp, xs, norm_mix[0:1], w_in_even[0], ca, cb, gmlp_vnorm[0:1])
    ws = gmlp_ws[0]
    cidx = jnp.arange(MIX_TM) // CHUNK
    w_p = jnp.where(cidx[:, None] >= cidx[None, :], ws, 0.0).astype(BF16)
    eye = jnp.eye(ITEMS_PER_TILE, dtype=F32)
    w_s = jnp.einsum('ab,gpq->gapbq', eye, ws[:, :DEC_SEQ, :DEC_SEQ]).reshape(B_GROUPS, MIX_TM, MIX_TM).astype(BF16)
    b_p = gmlp_bias[0].T
    b_s = jnp.tile(gmlp_bias[0][:, :DEC_SEQ], (1, ITEMS_PER_TILE)).T
    ab, k_s, v_s = _mixer(attn_sinks[0], q, k, v,
                          cache_swa_k[0].reshape(DEC_BATCH, WINDOW, KV_WIDTH),
                          cache_swa_v[0].reshape(DEC_BATCH, WINDOW, KV_WIDTH),
                          u, gv, w_p, w_s, b_p, b_s)
    h1, xn1 = _proj_ffn((xp, xs), ab, w_out_even[0], norm_ffn[0:1], w_ffn_up, w_ffn_down, 0,
                        norm_mix[1:2], final=False)

    tri_p = jnp.tril(jnp.ones((HG_CHUNK, HG_CHUNK), F32))
    seg = jnp.arange(HG_CHUNK) // DEC_SEQ
    tri_s = jnp.where(seg[:, None] == seg[None, :], tri_p, 0.0)
    og, s_p, s_s = _hgrn(xn1, w_in_odd[0], hgrn_lb, hgrn_onorm[0:1], tri_p.astype(BF16), tri_s.astype(BF16),
                         state_hgrn[0])
    y_p, y_s = _proj_ffn(h1, og, w_out_odd[0], norm_ffn[1:2], w_ffn_up, w_ffn_down, 1,
                         norm_final[None, :], final=True)

    kv_shape = (1, 1, WINDOW, A_KV_HEADS, HEAD_DIM)
    kv_s_shape = (1, DEC_BATCH, WINDOW, A_KV_HEADS, HEAD_DIM)
    return (y_p.reshape(1, N_PROMPT, D_MODEL),
            y_s.reshape(DEC_BATCH, DEC_SEQ, D_MODEL),
            k[N_PROMPT - WINDOW:N_PROMPT].reshape(kv_shape),
            v[N_PROMPT - WINDOW:N_PROMPT].reshape(kv_shape),
            k_s.reshape(kv_s_shape),
            v_s.reshape(kv_s_shape),
            gv[N_PROMPT:].reshape(1, DEC_BATCH, DEC_SEQ, B_GROUPS, GMLP_DIM),
            s_p[None, None],
            s_s[None])
```

```python
import functools
import math

import jax
import jax.numpy as jnp
from jax import lax
from jax.experimental import pallas as pl
from jax.experimental.pallas import tpu as pltpu

F32 = jnp.float32
BF16 = jnp.bfloat16

D_MODEL = 2048
N_PROMPT = 8192
DEC_BATCH = 16
DEC_SEQ = 32
N_SAMPLE = DEC_BATCH * DEC_SEQ
N_TOK = N_PROMPT + N_SAMPLE
PAST_LEN = 2048
EPS = 1e-6

HEAD_DIM = 64
A_HEADS = 16
A_KV_HEADS = 4
A_WIDTH = A_HEADS * HEAD_DIM
KV_WIDTH = A_KV_HEADS * HEAD_DIM
ROT_DIM = 16
ROPE_THETA = 500000.0
WINDOW = 128
CHUNK = 64
B_GROUPS = 8
GMLP_DIM = 128
B_WIDTH = B_GROUPS * GMLP_DIM
EVEN_IN = A_WIDTH + 2 * KV_WIDTH + 2 * B_WIDTH
C_HEADS = 16
C_DIM = 128
D_FF = 4 * D_MODEL

LANES = 128
TM = 1024
TMB = 1088
ROW_CHUNK = 272
N_TILES = N_TOK // TMB
TAIL = N_PROMPT - (N_TILES - 1) * TMB
MIX_TM = 128
HG_CHUNK = 128
IN_BLOCK = 512
PJ_BLOCK = 256
FF_BLOCK = 512
MXU_N = 256
VMEM_LIMIT = 58 * 1024 * 1024

_NT = (((1,), (1,)), ((), ()))


def _params(sem):
    return pltpu.CompilerParams(dimension_semantics=sem, vmem_limit_bytes=VMEM_LIMIT)


def _rms(x, g):
    return x * lax.rsqrt(jnp.mean(x * x, axis=-1, keepdims=True) + EPS) * g


def _sigmoid(x):
    return 0.5 * jnp.tanh(0.5 * x) + 0.5


def _gelu(x):
    c = math.sqrt(2.0 / math.pi)
    return x * (0.5 * (1.0 + jnp.tanh(c * (x + 0.044715 * (x * x * x)))))


def _dot(a, b):
    return jnp.dot(a, b, preferred_element_type=F32)


def _row_spans(start, stop, step=ROW_CHUNK):
    return [(r, min(r + step, stop)) for r in range(start, stop, step)]


def _even_in_kernel(xp_ref, xs_ref, g_ref, w_ref, ca_ref, cb_ref, gv_g_ref,
                    q_ref, k_ref, v_ref, u_ref, gv_ref, xn_ref):
    i = pl.program_id(0)
    j = pl.program_id(1)

    def norm_rows(src_ref, src0, dst0, n):
        for a, b in _row_spans(0, n):
            xn_ref[dst0 + a:dst0 + b, :] = _rms(src_ref[src0 + a:src0 + b, :], g_ref[...]).astype(BF16)

    @pl.when((j == 0) & (i < N_TILES - 1))
    def _():
        norm_rows(xp_ref, 0, 0, TMB)

    @pl.when((j == 0) & (i == N_TILES - 1))
    def _():
        norm_rows(xp_ref, 0, 0, TAIL)
        norm_rows(xs_ref, 0, TAIL, N_SAMPLE)

    z = _dot(xn_ref[...], w_ref[...].astype(BF16))

    def rope(x):
        lane = lax.broadcasted_iota(jnp.int32, (TMB, LANES), 1)
        first = (lane & (HEAD_DIM - 1)) < (ROT_DIM // 2)
        partner = jnp.where(first, pltpu.roll(x, LANES - ROT_DIM // 2, 1), pltpu.roll(x, ROT_DIM // 2, 1))
        return x * ca_ref[...] + partner * cb_ref[...]

    tiles = [slice(t * LANES, (t + 1) * LANES) for t in range(IN_BLOCK // LANES)]

    @pl.when(j < A_WIDTH // IN_BLOCK)
    def _():
        for sl in tiles:
            q_ref[:, sl] = (rope(z[:, sl]) * (HEAD_DIM ** -0.5)).astype(BF16)

    @pl.when(j == A_WIDTH // IN_BLOCK)
    def _():
        for sl in tiles[:KV_WIDTH // LANES]:
            k_ref[:, sl] = rope(z[:, sl])
        v_ref[...] = z[:, KV_WIDTH:]

    @pl.when((j > A_WIDTH // IN_BLOCK) & (j <= (A_WIDTH + B_WIDTH) // IN_BLOCK))
    def _():
        u_ref[...] = _gelu(z).astype(BF16)

    @pl.when(j > (A_WIDTH + B_WIDTH) // IN_BLOCK)
    def _():
        for sl in tiles:
            x = _gelu(z[:, sl])
            xc = x - jnp.mean(x, axis=-1, keepdims=True)
            y = xc * lax.rsqrt(jnp.mean(xc * xc, axis=-1, keepdims=True) + EPS)
            gv_ref[:, sl] = y * gv_g_ref[:, sl]


def _even_in(x_prompt, x_sample, g, w, ca, cb, gv_g):
    nq = A_WIDTH // IN_BLOCK
    nu = nq + 1
    nv = nu + B_WIDTH // IN_BLOCK
    row = lambda i, j: (i, 0)
    const = lambda i, j: (0, 0)
    return pl.pallas_call(
        _even_in_kernel,
        grid=(N_TILES, EVEN_IN // IN_BLOCK),
        in_specs=[
            pl.BlockSpec((TMB, D_MODEL), row),
            pl.BlockSpec((N_SAMPLE, D_MODEL), const),
            pl.BlockSpec((1, D_MODEL), const),
            pl.BlockSpec((D_MODEL, IN_BLOCK), lambda i, j: (0, j)),
            pl.BlockSpec((TMB, LANES), row),
            pl.BlockSpec((TMB, LANES), row),
            pl.BlockSpec((1, IN_BLOCK), lambda i, j: (0, jnp.maximum(j - nv, 0))),
        ],
        out_specs=[
            pl.BlockSpec((TMB, IN_BLOCK), lambda i, j: (i, jnp.minimum(j, nq - 1))),
            pl.BlockSpec((TMB, KV_WIDTH), row),
            pl.BlockSpec((TMB, KV_WIDTH), row),
            pl.BlockSpec((TMB, IN_BLOCK), lambda i, j: (i, jnp.clip(j - nu, 0, nv - nu - 1))),
            pl.BlockSpec((TMB, IN_BLOCK), lambda i, j: (i, jnp.maximum(j - nv, 0))),
        ],
        out_shape=[
            jax.ShapeDtypeStruct((N_TOK, A_WIDTH), BF16),
            jax.ShapeDtypeStruct((N_TOK, KV_WIDTH), F32),
            jax.ShapeDtypeStruct((N_TOK, KV_WIDTH), F32),
            jax.ShapeDtypeStruct((N_TOK, B_WIDTH), BF16),
            jax.ShapeDtypeStruct((N_TOK, B_WIDTH), F32),
        ],
        scratch_shapes=[pltpu.VMEM((TMB, D_MODEL), BF16)],
        compiler_params=_params(("arbitrary", "arbitrary")),
        name="even_in",
    )(x_prompt, x_sample, g, w, ca, cb, gv_g)


N_MIX_PROMPT = N_PROMPT // MIX_TM
ITEMS_PER_TILE = MIX_TM // DEC_SEQ
N_MIX = N_TOK // MIX_TM


def _spread_heads(x, kv_head):
    t = x[:, (kv_head // 2) * LANES:(kv_head // 2 + 1) * LANES]
    lane = lax.broadcasted_iota(jnp.int32, t.shape, 1)
    if kv_head % 2 == 0:
        lo = jnp.where(lane < HEAD_DIM, t, 0.0)
        hi = pltpu.roll(lo, HEAD_DIM, 1)
    else:
        hi = jnp.where(lane >= HEAD_DIM, t, 0.0)
        lo = pltpu.roll(hi, HEAD_DIM, 1)
    return lo, hi


def _kv_operands(kk, vv):
    s_len = kk[0].shape[0]
    kx = jnp.concatenate(kk, axis=0).astype(BF16)
    vx = jnp.concatenate(vv, axis=0)
    row = lax.broadcasted_iota(jnp.int32, vx.shape, 0)
    lane = lax.broadcasted_iota(jnp.int32, vx.shape, 1)
    own = ((row < s_len) & (lane < HEAD_DIM)) | ((row >= s_len) & (lane >= HEAD_DIM))
    ones = jnp.where(own, 1.0, 0.0)
    return kx, jnp.concatenate([vx, ones], axis=1).astype(BF16)


def _attend(q2, kx, vx1, sink_a, sink_b, allowed):
    s_len = kx.shape[0] // 2
    s = lax.dot_general(q2, kx, _NT, preferred_element_type=F32)
    ps, sink_terms = [], []
    for half, sink in ((0, sink_a), (1, sink_b)):
        sh = s[:, half * s_len:(half + 1) * s_len]
        if allowed is not None:
            sh = jnp.where(allowed, sh, -jnp.inf)
        m = jnp.maximum(jnp.max(sh, axis=-1, keepdims=True), sink)
        ps.append(jnp.exp(sh - m))
        sink_terms.append(jnp.exp(sink - m))
    o = _dot(jnp.concatenate(ps, axis=1).astype(BF16), vx1)
    lane = lax.broadcasted_iota(jnp.int32, (o.shape[0], LANES), 1)
    return o[:, :LANES] / (o[:, LANES:] + jnp.where(lane < HEAD_DIM, sink_terms[0], sink_terms[1]))


def _mixer_kernel(sink_ref, q_ref, kp_ref, kc_ref, vp_ref, vc_ref, ck_ref, cv_ref,
                  u_ref, gv_ref, wp_ref, ws_ref, bp_ref, bs_ref,
                  ab_ref, ko_ref, vo_ref):
    i = pl.program_id(0)

    def gmlp(w_ref, b_ref):
        for g in range(B_GROUPS):
            sl = slice(g * GMLP_DIM, (g + 1) * GMLP_DIM)
            mixed = _dot(w_ref[g], gv_ref[:, sl].astype(BF16)) + b_ref[:, g:g + 1]
            ab_ref[:, A_WIDTH + g * GMLP_DIM:A_WIDTH + (g + 1) * GMLP_DIM] = (
                u_ref[:, sl].astype(F32) * mixed).astype(BF16)

    @pl.when(i < N_MIX_PROMPT)
    def _prompt():
        kk = jnp.concatenate([kp_ref[...], kc_ref[...]], axis=0)
        vv = jnp.concatenate([vp_ref[...], vc_ref[...]], axis=0)
        row = lax.broadcasted_iota(jnp.int32, (MIX_TM, 2 * MIX_TM), 0)
        col = lax.broadcasted_iota(jnp.int32, (MIX_TM, 2 * MIX_TM), 1)
        allowed = ((row < CHUNK) & (col < 3 * CHUNK)) | ((row >= CHUNK) & (col >= CHUNK))
        allowed = allowed & ((col >= MIX_TM) | (i > 0))
        for j in range(A_KV_HEADS):
            kx, vx1 = _kv_operands(_spread_heads(kk, j), _spread_heads(vv, j))
            for p in (2 * j, 2 * j + 1):
                sl = slice(p * LANES, (p + 1) * LANES)
                o = _attend(q_ref[:, sl], kx, vx1, sink_ref[2 * p], sink_ref[2 * p + 1], allowed)
                ab_ref[:, sl] = o.astype(BF16)
        gmlp(wp_ref, bp_ref)

    @pl.when(i >= N_MIX_PROMPT)
    def _sample():
        for a in range(ITEMS_PER_TILE):
            rows = slice(a * DEC_SEQ, (a + 1) * DEC_SEQ)
            k_new = kc_ref[rows, :]
            v_new = vc_ref[rows, :]
            kk = jnp.concatenate([ck_ref[a], k_new], axis=0)
            vv = jnp.concatenate([cv_ref[a], v_new], axis=0)
            ko_ref[a] = kk[DEC_SEQ:, :]
            vo_ref[a] = vv[DEC_SEQ:, :]
            for j in range(A_KV_HEADS):
                kx, vx1 = _kv_operands(_spread_heads(kk, j), _spread_heads(vv, j))
                for p in (2 * j, 2 * j + 1):
                    sl = slice(p * LANES, (p + 1) * LANES)
                    o = _attend(q_ref[rows, sl], kx, vx1, sink_ref[2 * p], sink_ref[2 * p + 1], None)
                    ab_ref[rows, sl] = o.astype(BF16)
        gmlp(ws_ref, bs_ref)


def _mixer(sinks, q, k, v, cache_k, cache_v, u, gv, w_p, w_s, b_p, b_s):
    row = lambda i: (i, 0)
    prev = lambda i: (jnp.maximum(i - 1, 0), 0)
    item = lambda i: (jnp.maximum(i - N_MIX_PROMPT, 0), 0, 0)
    c2 = lambda i: (0, 0)
    c3 = lambda i: (0, 0, 0)
    cache_block = (ITEMS_PER_TILE, WINDOW, KV_WIDTH)
    return pl.pallas_call(
        _mixer_kernel,
        grid=(N_MIX,),
        in_specs=[
            pl.BlockSpec(memory_space=pltpu.SMEM),
            pl.BlockSpec((MIX_TM, A_WIDTH), row),
            pl.BlockSpec((MIX_TM, KV_WIDTH), prev),
            pl.BlockSpec((MIX_TM, KV_WIDTH), row),
            pl.BlockSpec((MIX_TM, KV_WIDTH), prev),
            pl.BlockSpec((MIX_TM, KV_WIDTH), row),
            pl.BlockSpec(cache_block, item),
            pl.BlockSpec(cache_block, item),
            pl.BlockSpec((MIX_TM, B_WIDTH), row),
            pl.BlockSpec((MIX_TM, B_WIDTH), row),
            pl.BlockSpec((B_GROUPS, MIX_TM, MIX_TM), c3),
            pl.BlockSpec((B_GROUPS, MIX_TM, MIX_TM), c3),
            pl.BlockSpec((MIX_TM, B_GROUPS), c2),
            pl.BlockSpec((MIX_TM, B_GROUPS), c2),
        ],
        out_specs=[
            pl.BlockSpec((MIX_TM, A_WIDTH + B_WIDTH), row),
            pl.BlockSpec(cache_block, item),
            pl.BlockSpec(cache_block, item),
        ],
        out_shape=[
            jax.ShapeDtypeStruct((N_TOK, A_WIDTH + B_WIDTH), BF16),
            jax.ShapeDtypeStruct((DEC_BATCH, WINDOW, KV_WIDTH), F32),
            jax.ShapeDtypeStruct((DEC_BATCH, WINDOW, KV_WIDTH), F32),
        ],
        compiler_params=_params(("arbitrary",)),
        name="mixer",
    )(sinks, q, k, k, v, v, cache_k, cache_v, u, gv, w_p, w_s, b_p, b_s)


N_PJ = D_MODEL // PJ_BLOCK
N_FF = D_FF // FF_BLOCK


def _proj_ffn_kernel(*refs, split_in, final):
    refs = list(refs)
    h_refs = [refs.pop(0) for _ in range(2 if split_in else 1)]
    y_ref, wo_ref, gf_ref, wu_ref, wd_ref, gn_ref, o_ref = refs[:7]
    if final:
        ys_ref, xn_ref = refs[7:]
    else:
        (xn_ref,) = refs[7:]
    i = pl.program_id(0)
    s = pl.program_id(1)
    last_tile = i == N_TILES - 1
    cols = [slice(c * MXU_N, (c + 1) * MXU_N) for c in range(D_MODEL // MXU_N)]

    def accumulate(lhs, w):
        for sl in cols:
            o_ref[:, sl] += _dot(lhs, w[:, sl])

    @pl.when(s == 0)
    def _():
        o_ref[...] = jnp.zeros_like(o_ref)

    @pl.when(s < N_PJ)
    def _():
        accumulate(y_ref[...], wo_ref[...].astype(BF16))

    for c in range(N_PJ):
        sl = slice(c * PJ_BLOCK, (c + 1) * PJ_BLOCK)
        if split_in:
            hp_ref, hs_ref = h_refs

            @pl.when((s == c) & jnp.logical_not(last_tile))
            def _():
                o_ref[:, sl] += hp_ref[...]

            @pl.when((s == c) & last_tile)
            def _():
                o_ref[0:TAIL, sl] += hp_ref[0:TAIL, :]
                o_ref[TAIL:, sl] += hs_ref[...]
        else:
            @pl.when(s == c)
            def _():
                o_ref[:, sl] += h_refs[0][...]

    @pl.when(s == N_PJ)
    def _():
        for a, b in _row_spans(0, TMB):
            xn_ref[a:b, :] = _rms(o_ref[a:b, :], gf_ref[...]).astype(BF16)

    @pl.when(s >= N_PJ)
    def _():
        t = jnp.maximum(_dot(xn_ref[...], wu_ref[...].astype(BF16)), 0.0)
        accumulate((t * t).astype(BF16), wd_ref[...].astype(BF16))

    last_step = s == N_PJ + N_FF - 1
    if final:
        @pl.when(last_step & jnp.logical_not(last_tile))
        def _():
            for a, b in _row_spans(0, TMB):
                o_ref[a:b, :] = _rms(o_ref[a:b, :], gn_ref[...])

        @pl.when(last_step & last_tile)
        def _():
            for a, b in _row_spans(0, TAIL):
                o_ref[a:b, :] = _rms(o_ref[a:b, :], gn_ref[...])
            for a, b in _row_spans(0, N_SAMPLE):
                ys_ref[a:b, :] = _rms(o_ref[TAIL + a:TAIL + b, :], gn_ref[...])
    else:
        @pl.when(last_step)
        def _():
            for a, b in _row_spans(0, TMB):
                xn_ref[a:b, :] = _rms(o_ref[a:b, :], gn_ref[...]).astype(BF16)


def _proj_ffn(h, y, w_out, g_ffn, w_up, w_down, layer, g_next, final):
    split_in = isinstance(h, tuple)
    row = lambda i, s: (i, 0)
    const = lambda i, s: (0, 0)
    pj = lambda s: jnp.minimum(s, N_PJ - 1)
    ff = lambda s: jnp.maximum(s - N_PJ, 0)
    h_specs = [pl.BlockSpec((TMB, PJ_BLOCK), lambda i, s: (i, pj(s)))]
    if split_in:
        h_specs.append(pl.BlockSpec((N_SAMPLE, PJ_BLOCK), lambda i, s: (0, pj(s))))
    if final:
        out_specs = [pl.BlockSpec((TMB, D_MODEL), row), pl.BlockSpec((N_SAMPLE, D_MODEL), const)]
        out_shape = [jax.ShapeDtypeStruct((N_PROMPT, D_MODEL), F32),
                     jax.ShapeDtypeStruct((N_SAMPLE, D_MODEL), F32)]
        scratch = [pltpu.VMEM((TMB, D_MODEL), BF16)]
    else:
        out_specs = [pl.BlockSpec((TMB, D_MODEL), row), pl.BlockSpec((TMB, D_MODEL), row)]
        out_shape = [jax.ShapeDtypeStruct((N_TOK, D_MODEL), F32),
                     jax.ShapeDtypeStruct((N_TOK, D_MODEL), BF16)]
        scratch = []
    return pl.pallas_call(
        functools.partial(_proj_ffn_kernel, split_in=split_in, final=final),
        grid=(N_TILES, N_PJ + N_FF),
        in_specs=[
            *h_specs,
            pl.BlockSpec((TMB, PJ_BLOCK), lambda i, s: (i, pj(s))),
            pl.BlockSpec((PJ_BLOCK, D_MODEL), lambda i, s: (pj(s), 0)),
            pl.BlockSpec((1, D_MODEL), const),
            pl.BlockSpec((None, D_MODEL, FF_BLOCK), lambda i, s: (layer, 0, ff(s))),
            pl.BlockSpec((None, FF_BLOCK, D_MODEL), lambda i, s: (layer, ff(s), 0)),
            pl.BlockSpec((1, D_MODEL), const),
        ],
        out_specs=out_specs,
        out_shape=out_shape,
        scratch_shapes=scratch,
        compiler_params=_params(("arbitrary", "arbitrary")),
        name="proj_ffn_final" if final else "proj_ffn",
    )(*(h if split_in else (h,)), y, w_out, g_ffn, w_up, w_down, g_next)


N_HG_PROMPT = N_PROMPT // TM
N_HG = N_HG_PROMPT + 1
CHUNKS_PER_TILE = TM // HG_CHUNK
SAMPLE_CHUNKS = N_SAMPLE // HG_CHUNK
ITEMS_PER_CHUNK = HG_CHUNK // DEC_SEQ


def _hgrn_scores(q, k, f, b_ref, r0, seg):
    n = HG_CHUNK
    rows = lax.broadcasted_iota(jnp.int32, (n, n), 0)
    cols = lax.broadcasted_iota(jnp.int32, (n, n), 1)
    apart = rows ^ cols

    def coarse(h):
        qs, ks = [], []
        zero = jnp.zeros((h, C_DIM), F32)
        for g in range(n // (2 * h)):
            base = r0 + g * 2 * h
            beta = b_ref[base + h - 1:base + h, :]
            lo = slice(g * 2 * h, g * 2 * h + h)
            up = slice(g * 2 * h + h, (g + 1) * 2 * h)
            ks += [k[lo] * jnp.exp(beta - b_ref[base:base + h, :]), zero]
            qs.append(q[up] * jnp.exp(b_ref[base + h:base + 2 * h, :] - beta))
        return jnp.concatenate(qs, axis=0).astype(BF16), jnp.concatenate(ks, axis=0).astype(BF16)

    def spread(s_up, h):
        pieces = []
        for g in range(n // (2 * h)):
            pieces += [jnp.zeros((h, n), F32), s_up[g * h:(g + 1) * h]]
        return jnp.concatenate(pieces, axis=0)

    def fine(h):
        if h == 4:
            qs, ks = [], []
            upper = lax.broadcasted_iota(jnp.int32, (8, C_DIM), 0) >= 4
            for g in range(n // 8):
                sl = slice(8 * g, 8 * g + 8)
                base = r0 + 8 * g
                e = jnp.exp(-jnp.abs(b_ref[base:base + 8, :] - b_ref[base + 3:base + 4, :]))
                qs.append(jnp.where(upper, q[sl] * e, 0.0))
                ks.append(jnp.where(upper, 0.0, k[sl] * e))
            qt = jnp.concatenate(qs, axis=0)
            kt = jnp.concatenate(ks, axis=0)
        elif h == 2:
            r4 = lax.broadcasted_iota(jnp.int32, (n, C_DIM), 0) & 3
            f_next = pltpu.roll(f, n - 1, 0)
            f_prev = pltpu.roll(f, 1, 0)
            kt = jnp.where(r4 == 0, k * f_next, jnp.where(r4 == 1, k, 0.0))
            qt = jnp.where(r4 == 2, q * f, jnp.where(r4 == 3, q * f * f_prev, 0.0))
        else:
            odd = (lax.broadcasted_iota(jnp.int32, (n, C_DIM), 0) & 1) == 1
            kt = jnp.where(odd, 0.0, k)
            qt = jnp.where(odd, q * f, 0.0)
        return qt.astype(BF16), kt.astype(BF16)

    scores = None
    h = seg // 2
    while h >= 1:
        qt, kt = coarse(h) if h >= 8 else fine(h)
        s_h = lax.dot_general(qt, kt, _NT, preferred_element_type=F32)
        if h >= 8:
            s_h = spread(s_h, h)
        if scores is None:
            scores = s_h if 2 * h == n else jnp.where(apart < 2 * h, s_h, 0.0)
        else:
            scores = jnp.where(apart < 2 * h, s_h, scores)
        h //= 2
    return jnp.where(apart == 0, jnp.sum(q * k, axis=1, keepdims=True), scores)


def _hgrn_kernel(xn0_ref, xn_next_ref, xn_s_ref, wq_ref, wf_ref, wi_ref, wg_ref, lb_ref, go_ref, trip_ref,
                 tris_ref, s0_ref, og_ref, sp_ref, ss_ref, w_ref, z_ref, f_ref, b_ref, st_ref):
    t = pl.program_id(1)

    lbx = lb_ref[...]
    e = jnp.exp(lbx - jnp.max(lbx, axis=0, keepdims=True))
    p = e / jnp.sum(e, axis=0, keepdims=True)
    lb = (p[0:1, :] + p[1:2, :]) - p[0:1, :]

    @pl.when(t == 0)
    def _():
        st_ref[...] = jnp.zeros_like(st_ref)
        for c, src in enumerate((wq_ref, wf_ref, wi_ref, wg_ref)):
            w_ref[:, c * C_DIM:(c + 1) * C_DIM] = src[...].astype(BF16)
        z_ref[...] = _dot(xn0_ref[...], w_ref[...])

    def cumulative_log_decay(tri_ref, n_chunks):
        lf = jnp.log(f_ref[0:n_chunks * HG_CHUNK, :])
        lf = jnp.concatenate([lf[c * HG_CHUNK:(c + 1) * HG_CHUNK] for c in range(n_chunks)], axis=1)
        hi = lf.astype(BF16)
        lo = (lf - hi.astype(F32)).astype(BF16)
        bb = _dot(tri_ref[...], jnp.concatenate([hi, lo], axis=1))
        for c in range(n_chunks):
            lanes = slice(c * C_DIM, (c + 1) * C_DIM)
            b_ref[c * HG_CHUNK:(c + 1) * HG_CHUNK, :] = bb[:, lanes] + bb[:, n_chunks * C_DIM:][:, lanes]

    def chunk(c, seg):
        r0 = c * HG_CHUNK
        rows = slice(r0, r0 + HG_CHUNK)
        zq = z_ref[rows, 0:C_DIM]
        f = f_ref[rows, :]
        v = z_ref[rows, 2 * C_DIM:3 * C_DIM]
        gate = _sigmoid(z_ref[rows, 3 * C_DIM:4 * C_DIM])
        q = zq * _sigmoid(zq)
        k = 1.0 - f
        scores = _hgrn_scores(q, k, f, b_ref, r0, seg)
        o = _dot(scores.astype(BF16), v.astype(BF16))
        b = b_ref[rows, :]
        qe = (q * jnp.exp(b)).astype(BF16)
        if seg == HG_CHUNK:
            s_t = st_ref[...]
            o = o + lax.dot_general(qe, s_t.astype(BF16), _NT, preferred_element_type=F32)
            b_last = b_ref[r0 + HG_CHUNK - 1:r0 + HG_CHUNK, :]
            ke = (k * jnp.exp(b_last - b)).astype(BF16)
            st_ref[...] = s_t * jnp.exp(b_last) + _dot(v.T.astype(BF16), ke)
        else:
            outs = []
            v_t = v.T.astype(BF16)
            row = lax.broadcasted_iota(jnp.int32, (HG_CHUNK, C_DIM), 0)
            for a in range(HG_CHUNK // seg):
                item = c * (HG_CHUNK // seg) + a
                r = slice(a * seg, (a + 1) * seg)
                s_t = s0_ref[item].T
                outs.append(lax.dot_general(qe[r], s_t.astype(BF16), _NT, preferred_element_type=F32))
                b_last = b_ref[r0 + (a + 1) * seg - 1:r0 + (a + 1) * seg, :]
                own = (row >= a * seg) & (row < (a + 1) * seg)
                ke = jnp.where(own, k * jnp.exp(jnp.minimum(b_last - b, 0.0)), 0.0).astype(BF16)
                s_new = s_t * jnp.exp(b_last) + _dot(v_t, ke)
                ss_ref[item] = s_new.T
            o = o + jnp.concatenate(outs, axis=0)
        y = _rms(o, go_ref[...]) * gate
        og_ref[rows, :] = y.astype(BF16)

    def tile(tri_ref, seg, n_chunks, project_next):
        rows = slice(0, n_chunks * HG_CHUNK)
        f_ref[rows, :] = lb + (1.0 - lb) * _sigmoid(z_ref[rows, C_DIM:2 * C_DIM])
        cumulative_log_decay(tri_ref, n_chunks)
        z_next = []
        for c in range(n_chunks):
            chunk(c, seg)
            if project_next:
                z_next.append(_dot(xn_next_ref[c * HG_CHUNK:(c + 1) * HG_CHUNK, :], w_ref[...]))
        if project_next:
            z_ref[...] = jnp.concatenate(z_next, axis=0)

    @pl.when(t < N_HG_PROMPT)
    def _():
        tile(trip_ref, HG_CHUNK, CHUNKS_PER_TILE, True)

    @pl.when(t == N_HG_PROMPT - 1)
    def _():
        sp_ref[...] = st_ref[...].T

    @pl.when(t == N_HG_PROMPT)
    def _():
        z_ref[0:N_SAMPLE, :] = _dot(xn_s_ref[...], w_ref[...])
        tile(tris_ref, DEC_SEQ, SAMPLE_CHUNKS, False)


def _hgrn(xn, w_in, lb, g_onorm, tri_p, tri_s, state0):
    c2 = lambda j, t: (0, 0)
    head_col = lambda j, t: (0, j)
    w_specs = [pl.BlockSpec((D_MODEL, C_DIM), functools.partial(lambda j, t, c: (0, c * C_HEADS + j), c=c))
               for c in range(4)]
    return pl.pallas_call(
        _hgrn_kernel,
        grid=(C_HEADS, N_HG),
        in_specs=[
            pl.BlockSpec((TM, D_MODEL), c2),
            pl.BlockSpec((TM, D_MODEL), lambda j, t: (jnp.minimum(t + 1, N_HG_PROMPT - 1), 0)),
            pl.BlockSpec((N_SAMPLE, D_MODEL), lambda j, t: (N_PROMPT // N_SAMPLE, 0)),
            *w_specs,
            pl.BlockSpec((2, C_DIM), head_col),
            pl.BlockSpec((1, C_DIM), head_col),
            pl.BlockSpec((HG_CHUNK, HG_CHUNK), c2),
            pl.BlockSpec((HG_CHUNK, HG_CHUNK), c2),
            pl.BlockSpec((DEC_BATCH, None, C_DIM, C_DIM), lambda j, t: (0, j, 0, 0)),
        ],
        out_specs=[
            pl.BlockSpec((TM, C_DIM), lambda j, t: (t, j)),
            pl.BlockSpec((None, C_DIM, C_DIM), lambda j, t: (j, 0, 0)),
            pl.BlockSpec((DEC_BATCH, None, C_DIM, C_DIM), lambda j, t: (0, j, 0, 0)),
        ],
        out_shape=[
            jax.ShapeDtypeStruct((N_TOK, C_HEADS * C_DIM), BF16),
            jax.ShapeDtypeStruct((C_HEADS, C_DIM, C_DIM), F32),
            jax.ShapeDtypeStruct((DEC_BATCH, C_HEADS, C_DIM, C_DIM), F32),
        ],
        scratch_shapes=[
            pltpu.VMEM((D_MODEL, 4 * C_DIM), BF16),
            pltpu.VMEM((TM, 4 * C_DIM), F32),
            pltpu.VMEM((TM, C_DIM), F32),
            pltpu.VMEM((TM, C_DIM), F32),
            pltpu.VMEM((C_DIM, C_DIM), F32),
        ],
        compiler_params=_params(("arbitrary", "arbitrary")),
        name="hgrn",
    )(xn, xn, xn, w_in, w_in, w_in, w_in, lb, g_onorm, tri_p, tri_s, state0)


def _rope_tables():
    half = ROT_DIM // 2
    inv = ROPE_THETA ** (-(jnp.arange(half, dtype=F32) * (2.0 / ROT_DIM)))
    pos = jnp.concatenate([jnp.arange(N_PROMPT, dtype=jnp.int32),
                           jnp.tile(PAST_LEN + jnp.arange(DEC_SEQ, dtype=jnp.int32), DEC_BATCH)])
    ang = pos.astype(F32)[:, None] * inv[None, :]
    cos, sin = jnp.cos(ang), jnp.sin(ang)
    ones = jnp.ones((N_TOK, HEAD_DIM - ROT_DIM), F32)
    ca = jnp.concatenate([cos, cos, ones], axis=1)
    cb = jnp.concatenate([-sin, sin, 0.0 * ones], axis=1)
    return jnp.tile(ca, (1, LANES // HEAD_DIM)), jnp.tile(cb, (1, LANES // HEAD_DIM))


def kernel(x_prompt, x_sample, cache_swa_k, cache_swa_v, state_hgrn, norm_mix, norm_ffn, norm_final,
           w_in_even, w_out_even, attn_sinks, gmlp_vnorm, gmlp_ws, gmlp_bias, w_in_odd, w_out_odd,
           hgrn_lb, hgrn_onorm, w_ffn_up, w_ffn_down):
    xp = x_prompt.reshape(N_PROMPT, D_MODEL)
    xs = x_sample.reshape(N_SAMPLE, D_MODEL)
    ca, cb = _rope_tables()

    q, k, v, u, gv = _even_in(xp, xs, norm_mix[0:1], w_in_even[0], ca, cb, gmlp_vnorm[0:1])
    ws = gmlp_ws[0]
    cidx = jnp.arange(MIX_TM) // CHUNK
    w_p = jnp.where(cidx[:, None] >= cidx[None, :], ws, 0.0).astype(BF16)
    eye = jnp.eye(ITEMS_PER_TILE, dtype=F32)
    w_s = jnp.einsum('ab,gpq->gapbq', eye, ws[:, :DEC_SEQ, :DEC_SEQ]).reshape(B_GROUPS, MIX_TM, MIX_TM).astype(BF16)
    b_p = gmlp_bias[0].T
    b_s = jnp.tile(gmlp_bias[0][:, :DEC_SEQ], (1, ITEMS_PER_TILE)).T
    ab, k_s, v_s = _mixer(attn_sinks[0], q, k, v,
                          cache_swa_k[0].reshape(DEC_BATCH, WINDOW, KV_WIDTH),
                          cache_swa_v[0].reshape(DEC_BATCH, WINDOW, KV_WIDTH),
                          u, gv, w_p, w_s, b_p, b_s)
    h1, xn1 = _proj_ffn((xp, xs), ab, w_out_even[0], norm_ffn[0:1], w_ffn_up, w_ffn_down, 0,
                        norm_mix[1:2], final=False)

    tri_p = jnp.tril(jnp.ones((HG_CHUNK, HG_CHUNK), F32))
    seg = jnp.arange(HG_CHUNK) // DEC_SEQ
    tri_s = jnp.where(seg[:, None] == seg[None, :], tri_p, 0.0)
    og, s_p, s_s = _hgrn(xn1, w_in_odd[0], hgrn_lb, hgrn_onorm[0:1], tri_p.astype(BF16), tri_s.astype(BF16),
                         state_hgrn[0])
    y_p, y_s = _proj_ffn(h1, og, w_out_odd[0], norm_ffn[1:2], w_ffn_up, w_ffn_down, 1,
                         norm_final[None, :], final=True)

    kv_shape = (1, 1, WINDOW, A_KV_HEADS, HEAD_DIM)
    kv_s_shape = (1, DEC_BATCH, WINDOW, A_KV_HEADS, HEAD_DIM)
    return (y_p.reshape(1, N_PROMPT, D_MODEL),
            y_s.reshape(DEC_BATCH, DEC_SEQ, D_MODEL),
            k[N_PROMPT - WINDOW:N_PROMPT].reshape(kv_shape),
            v[N_PROMPT - WINDOW:N_PROMPT].reshape(kv_shape),
            k_s.reshape(kv_s_shape),
            v_s.reshape(kv_s_shape),
            gv[N_PROMPT:].reshape(1, DEC_BATCH, DEC_SEQ, B_GROUPS, GMLP_DIM),
            s_p[None, None],
            s_s[None])
```

```python
import functools
import math

import jax
import jax.numpy as jnp
from jax import lax
from jax.experimental import pallas as pl
from jax.experimental.pallas import tpu as pltpu

F32 = jnp.float32
BF16 = jnp.bfloat16

D_MODEL = 2048
N_PROMPT = 8192
DEC_BATCH = 16
DEC_SEQ = 32
N_SAMPLE = DEC_BATCH * DEC_SEQ
N_TOK = N_PROMPT + N_SAMPLE
PAST_LEN = 2048
EPS = 1e-6

HEAD_DIM = 64
A_HEADS = 16
A_KV_HEADS = 4
A_WIDTH = A_HEADS * HEAD_DIM
KV_WIDTH = A_KV_HEADS * HEAD_DIM
ROT_DIM = 16
ROPE_THETA = 500000.0
WINDOW = 128
CHUNK = 64
B_GROUPS = 8
GMLP_DIM = 128
B_WIDTH = B_GROUPS * GMLP_DIM
EVEN_IN = A_WIDTH + 2 * KV_WIDTH + 2 * B_WIDTH
C_HEADS = 16
C_DIM = 128
D_FF = 4 * D_MODEL

LANES = 128
TM = 1024
TMB = 1088
ROW_CHUNK = 272
N_TILES = N_TOK // TMB
TAIL = N_PROMPT - (N_TILES - 1) * TMB
MIX_TM = 128
HG_CHUNK = 128
IN_BLOCK = 512
PJ_BLOCK = 256
FF_BLOCK = 512
MXU_N = 256
VMEM_LIMIT = 58 * 1024 * 1024

_NT = (((1,), (1,)), ((), ()))


def _params(sem):
    return pltpu.CompilerParams(dimension_semantics=sem, vmem_limit_bytes=VMEM_LIMIT)


def _rms(x, g):
    return x * lax.rsqrt(jnp.mean(x * x, axis=-1, keepdims=True) + EPS) * g


def _sigmoid(x):
    return 0.5 * jnp.tanh(0.5 * x) + 0.5


def _gelu(x):
    c = math.sqrt(2.0 / math.pi)
    return x * (0.5 * (1.0 + jnp.tanh(c * (x + 0.044715 * (x * x * x)))))


def _dot(a, b):
    return jnp.dot(a, b, preferred_element_type=F32)


def _row_spans(start, stop, step=ROW_CHUNK):
    return [(r, min(r + step, stop)) for r in range(start, stop, step)]


def _even_in_kernel(xp_ref, xs_ref, g_ref, w_ref, ca_ref, cb_ref, gv_g_ref,
                    q_ref, k_ref, v_ref, u_ref, gv_ref, xn_ref, z_ref):
    i = pl.program_id(0)
    j = pl.program_id(1)

    def norm_rows(src_ref, src0, dst0, n):
        for a, b in _row_spans(0, n):
            xn_ref[dst0 + a:dst0 + b, :] = _rms(src_ref[src0 + a:src0 + b, :], g_ref[...]).astype(BF16)

    @pl.when((j == 0) & (i < N_TILES - 1))
    def _():
        norm_rows(xp_ref, 0, 0, TMB)

    @pl.when((j == 0) & (i == N_TILES - 1))
    def _():
        norm_rows(xp_ref, 0, 0, TAIL)
        norm_rows(xs_ref, 0, TAIL, N_SAMPLE)

    def rope(x):
        lane = lax.broadcasted_iota(jnp.int32, (TMB, LANES), 1)
        first = (lane & (HEAD_DIM - 1)) < (ROT_DIM // 2)
        partner = jnp.where(first, pltpu.roll(x, LANES - ROT_DIM // 2, 1), pltpu.roll(x, ROT_DIM // 2, 1))
        return x * ca_ref[...] + partner * cb_ref[...]

    tiles = [slice(t * LANES, (t + 1) * LANES) for t in range(IN_BLOCK // LANES)]

    def epilogue(c):
        if c < IN_Q:
            for sl in tiles:
                q_ref[:, sl] = (rope(z_ref[:, sl]) * (HEAD_DIM ** -0.5)).astype(BF16)
        elif c == IN_Q:
            for sl in tiles[:KV_WIDTH // LANES]:
                k_ref[:, sl] = rope(z_ref[:, sl])
            v_ref[...] = z_ref[:, KV_WIDTH:]
        elif c < IN_V:
            u_ref[...] = _gelu(z_ref[...]).astype(BF16)
        else:
            for sl in tiles:
                x = _gelu(z_ref[:, sl])
                xc = x - jnp.mean(x, axis=-1, keepdims=True)
                y = xc * lax.rsqrt(jnp.mean(xc * xc, axis=-1, keepdims=True) + EPS)
                gv_ref[:, sl] = y * gv_g_ref[:, sl]

    for c in range(IN_BLOCKS + 1):
        @pl.when(j == c)
        def _():
            if c < IN_BLOCKS:
                z = _dot(xn_ref[...], w_ref[...].astype(BF16))
            if c > 0:
                epilogue(c - 1)
            if c < IN_BLOCKS:
                z_ref[...] = z


IN_BLOCKS = EVEN_IN // IN_BLOCK
IN_Q = A_WIDTH // IN_BLOCK
IN_U = IN_Q + 1
IN_V = IN_U + B_WIDTH // IN_BLOCK


def _even_in(x_prompt, x_sample, g, w, ca, cb, gv_g):
    row = lambda i, j: (i, 0)
    const = lambda i, j: (0, 0)
    done = lambda j, first, count: jnp.clip(j - 1 - first, 0, count - 1)
    return pl.pallas_call(
        _even_in_kernel,
        grid=(N_TILES, IN_BLOCKS + 1),
        in_specs=[
            pl.BlockSpec((TMB, D_MODEL), row),
            pl.BlockSpec((N_SAMPLE, D_MODEL), const),
            pl.BlockSpec((1, D_MODEL), const),
            pl.BlockSpec((D_MODEL, IN_BLOCK), lambda i, j: (0, jnp.minimum(j, IN_BLOCKS - 1))),
            pl.BlockSpec((TMB, LANES), row),
            pl.BlockSpec((TMB, LANES), row),
            pl.BlockSpec((1, IN_BLOCK), lambda i, j: (0, done(j, IN_V, IN_BLOCKS - IN_V))),
        ],
        out_specs=[
            pl.BlockSpec((TMB, IN_BLOCK), lambda i, j: (i, done(j, 0, IN_Q))),
            pl.BlockSpec((TMB, KV_WIDTH), row),
            pl.BlockSpec((TMB, KV_WIDTH), row),
            pl.BlockSpec((TMB, IN_BLOCK), lambda i, j: (i, done(j, IN_U, IN_V - IN_U))),
            pl.BlockSpec((TMB, IN_BLOCK), lambda i, j: (i, done(j, IN_V, IN_BLOCKS - IN_V))),
        ],
        out_shape=[
            jax.ShapeDtypeStruct((N_TOK, A_WIDTH), BF16),
            jax.ShapeDtypeStruct((N_TOK, KV_WIDTH), F32),
            jax.ShapeDtypeStruct((N_TOK, KV_WIDTH), F32),
            jax.ShapeDtypeStruct((N_TOK, B_WIDTH), BF16),
            jax.ShapeDtypeStruct((N_TOK, B_WIDTH), F32),
        ],
        scratch_shapes=[pltpu.VMEM((TMB, D_MODEL), BF16), pltpu.VMEM((TMB, IN_BLOCK), F32)],
        compiler_params=_params(("arbitrary", "arbitrary")),
        name="even_in",
    )(x_prompt, x_sample, g, w, ca, cb, gv_g)


N_MIX_PROMPT = N_PROMPT // MIX_TM
ITEMS_PER_TILE = MIX_TM // DEC_SEQ
N_MIX = N_TOK // MIX_TM


def _spread_heads(x, kv_head):
    t = x[:, (kv_head // 2) * LANES:(kv_head // 2 + 1) * LANES]
    lane = lax.broadcasted_iota(jnp.int32, t.shape, 1)
    if kv_head % 2 == 0:
        lo = jnp.where(lane < HEAD_DIM, t, 0.0)
        hi = pltpu.roll(lo, HEAD_DIM, 1)
    else:
        hi = jnp.where(lane >= HEAD_DIM, t, 0.0)
        lo = pltpu.roll(hi, HEAD_DIM, 1)
    return lo, hi


def _kv_operands(kk, vv):
    s_len = kk[0].shape[0]
    kx = jnp.concatenate(kk, axis=0).astype(BF16)
    vx = jnp.concatenate(vv, axis=0)
    row = lax.broadcasted_iota(jnp.int32, vx.shape, 0)
    lane = lax.broadcasted_iota(jnp.int32, vx.shape, 1)
    own = ((row < s_len) & (lane < HEAD_DIM)) | ((row >= s_len) & (lane >= HEAD_DIM))
    ones = jnp.where(own, 1.0, 0.0)
    return kx, jnp.concatenate([vx, ones], axis=1).astype(BF16)


def _attend(q2, kx, vx1, sink_a, sink_b, allowed):
    s_len = kx.shape[0] // 2
    s = lax.dot_general(q2, kx, _NT, preferred_element_type=F32)
    ps, sink_terms = [], []
    for half, sink in ((0, sink_a), (1, sink_b)):
        sh = s[:, half * s_len:(half + 1) * s_len]
        if allowed is not None:
            sh = jnp.where(allowed, sh, -jnp.inf)
        m = jnp.maximum(jnp.max(sh, axis=-1, keepdims=True), sink)
        ps.append(jnp.exp(sh - m))
        sink_terms.append(jnp.exp(sink - m))
    o = _dot(jnp.concatenate(ps, axis=1).astype(BF16), vx1)
    lane = lax.broadcasted_iota(jnp.int32, (o.shape[0], LANES), 1)
    return o[:, :LANES] / (o[:, LANES:] + jnp.where(lane < HEAD_DIM, sink_terms[0], sink_terms[1]))


def _mixer_kernel(sink_ref, q_ref, kp_ref, kc_ref, vp_ref, vc_ref, ck_ref, cv_ref,
                  u_ref, gv_ref, wp_ref, ws_ref, bp_ref, bs_ref,
                  ab_ref, ko_ref, vo_ref):
    i = pl.program_id(0)

    def gmlp(w_ref, b_ref):
        for g in range(B_GROUPS):
            sl = slice(g * GMLP_DIM, (g + 1) * GMLP_DIM)
            mixed = _dot(w_ref[g], gv_ref[:, sl].astype(BF16)) + b_ref[:, g:g + 1]
            ab_ref[:, A_WIDTH + g * GMLP_DIM:A_WIDTH + (g + 1) * GMLP_DIM] = (
                u_ref[:, sl].astype(F32) * mixed).astype(BF16)

    @pl.when(i < N_MIX_PROMPT)
    def _prompt():
        kk = jnp.concatenate([kp_ref[...], kc_ref[...]], axis=0)
        vv = jnp.concatenate([vp_ref[...], vc_ref[...]], axis=0)
        row = lax.broadcasted_iota(jnp.int32, (MIX_TM, 2 * MIX_TM), 0)
        col = lax.broadcasted_iota(jnp.int32, (MIX_TM, 2 * MIX_TM), 1)
        allowed = ((row < CHUNK) & (col < 3 * CHUNK)) | ((row >= CHUNK) & (col >= CHUNK))
        allowed = allowed & ((col >= MIX_TM) | (i > 0))
        for j in range(A_KV_HEADS):
            kx, vx1 = _kv_operands(_spread_heads(kk, j), _spread_heads(vv, j))
            for p in (2 * j, 2 * j + 1):
                sl = slice(p * LANES, (p + 1) * LANES)
                o = _attend(q_ref[:, sl], kx, vx1, sink_ref[2 * p], sink_ref[2 * p + 1], allowed)
                ab_ref[:, sl] = o.astype(BF16)
        gmlp(wp_ref, bp_ref)

    @pl.when(i >= N_MIX_PROMPT)
    def _sample():
        for a in range(ITEMS_PER_TILE):
            rows = slice(a * DEC_SEQ, (a + 1) * DEC_SEQ)
            k_new = kc_ref[rows, :]
            v_new = vc_ref[rows, :]
            kk = jnp.concatenate([ck_ref[a], k_new], axis=0)
            vv = jnp.concatenate([cv_ref[a], v_new], axis=0)
            ko_ref[a] = kk[DEC_SEQ:, :]
            vo_ref[a] = vv[DEC_SEQ:, :]
            for j in range(A_KV_HEADS):
                kx, vx1 = _kv_operands(_spread_heads(kk, j), _spread_heads(vv, j))
                for p in (2 * j, 2 * j + 1):
                    sl = slice(p * LANES, (p + 1) * LANES)
                    o = _attend(q_ref[rows, sl], kx, vx1, sink_ref[2 * p], sink_ref[2 * p + 1], None)
                    ab_ref[rows, sl] = o.astype(BF16)
        gmlp(ws_ref, bs_ref)


def _mixer(sinks, q, k, v, cache_k, cache_v, u, gv, w_p, w_s, b_p, b_s):
    row = lambda i: (i, 0)
    prev = lambda i: (jnp.maximum(i - 1, 0), 0)
    item = lambda i: (jnp.maximum(i - N_MIX_PROMPT, 0), 0, 0)
    c2 = lambda i: (0, 0)
    c3 = lambda i: (0, 0, 0)
    cache_block = (ITEMS_PER_TILE, WINDOW, KV_WIDTH)
    return pl.pallas_call(
        _mixer_kernel,
        grid=(N_MIX,),
        in_specs=[
            pl.BlockSpec(memory_space=pltpu.SMEM),
            pl.BlockSpec((MIX_TM, A_WIDTH), row),
            pl.BlockSpec((MIX_TM, KV_WIDTH), prev),
            pl.BlockSpec((MIX_TM, KV_WIDTH), row),
            pl.BlockSpec((MIX_TM, KV_WIDTH), prev),
            pl.BlockSpec((MIX_TM, KV_WIDTH), row),
            pl.BlockSpec(cache_block, item),
            pl.BlockSpec(cache_block, item),
            pl.BlockSpec((MIX_TM, B_WIDTH), row),
            pl.BlockSpec((MIX_TM, B_WIDTH), row),
            pl.BlockSpec((B_GROUPS, MIX_TM, MIX_TM), c3),
            pl.BlockSpec((B_GROUPS, MIX_TM, MIX_TM), c3),
            pl.BlockSpec((MIX_TM, B_GROUPS), c2),
            pl.BlockSpec((MIX_TM, B_GROUPS), c2),
        ],
        out_specs=[
            pl.BlockSpec((MIX_TM, A_WIDTH + B_WIDTH), row),
            pl.BlockSpec(cache_block, item),
            pl.BlockSpec(cache_block, item),
        ],
        out_shape=[
            jax.ShapeDtypeStruct((N_TOK, A_WIDTH + B_WIDTH), BF16),
            jax.ShapeDtypeStruct((DEC_BATCH, WINDOW, KV_WIDTH), F32),
            jax.ShapeDtypeStruct((DEC_BATCH, WINDOW, KV_WIDTH), F32),
        ],
        compiler_params=_params(("arbitrary",)),
        name="mixer",
    )(sinks, q, k, k, v, v, cache_k, cache_v, u, gv, w_p, w_s, b_p, b_s)


N_PJ = D_MODEL // PJ_BLOCK
N_FF = D_FF // FF_BLOCK


def _proj_ffn_kernel(*refs, split_in, final):
    refs = list(refs)
    h_refs = [refs.pop(0) for _ in range(2 if split_in else 1)]
    y_ref, wo_ref, gf_ref, wu_ref, wd_ref, gn_ref, o_ref = refs[:7]
    if final:
        ys_ref, xn_ref = refs[7:]
    else:
        (xn_ref,) = refs[7:]
    i = pl.program_id(0)
    s = pl.program_id(1)
    last_tile = i == N_TILES - 1
    cols = [slice(c * MXU_N, (c + 1) * MXU_N) for c in range(D_MODEL // MXU_N)]

    def accumulate(lhs, w):
        for sl in cols:
            o_ref[:, sl] += _dot(lhs, w[:, sl])

    @pl.when(s == 0)
    def _():
        o_ref[...] = jnp.zeros_like(o_ref)

    @pl.when(s < N_PJ)
    def _():
        accumulate(y_ref[...], wo_ref[...].astype(BF16))

    for c in range(N_PJ):
        sl = slice(c * PJ_BLOCK, (c + 1) * PJ_BLOCK)
        if split_in:
            hp_ref, hs_ref = h_refs

            @pl.when((s == c) & jnp.logical_not(last_tile))
            def _():
                o_ref[:, sl] += hp_ref[...]

            @pl.when((s == c) & last_tile)
            def _():
                o_ref[0:TAIL, sl] += hp_ref[0:TAIL, :]
                o_ref[TAIL:, sl] += hs_ref[...]
        else:
            @pl.when(s == c)
            def _():
                o_ref[:, sl] += h_refs[0][...]

    @pl.when(s == N_PJ)
    def _():
        for a, b in _row_spans(0, TMB):
            xn_ref[a:b, :] = _rms(o_ref[a:b, :], gf_ref[...]).astype(BF16)

    @pl.when(s >= N_PJ)
    def _():
        t = jnp.maximum(_dot(xn_ref[...], wu_ref[...].astype(BF16)), 0.0)
        accumulate((t * t).astype(BF16), wd_ref[...].astype(BF16))

    last_step = s == N_PJ + N_FF - 1
    if final:
        @pl.when(last_step & jnp.logical_not(last_tile))
        def _():
            for a, b in _row_spans(0, TMB):
                o_ref[a:b, :] = _rms(o_ref[a:b, :], gn_ref[...])

        @pl.when(last_step & last_tile)
        def _():
            for a, b in _row_spans(0, TAIL):
                o_ref[a:b, :] = _rms(o_ref[a:b, :], gn_ref[...])
            for a, b in _row_spans(0, N_SAMPLE):
                ys_ref[a:b, :] = _rms(o_ref[TAIL + a:TAIL + b, :], gn_ref[...])
    else:
        @pl.when(last_step)
        def _():
            for a, b in _row_spans(0, TMB):
                xn_ref[a:b, :] = _rms(o_ref[a:b, :], gn_ref[...]).astype(BF16)


def _proj_ffn(h, y, w_out, g_ffn, w_up, w_down, layer, g_next, final):
    split_in = isinstance(h, tuple)
    row = lambda i, s: (i, 0)
    const = lambda i, s: (0, 0)
    pj = lambda s: jnp.minimum(s, N_PJ - 1)
    ff = lambda s: jnp.maximum(s - N_PJ, 0)
    h_specs = [pl.BlockSpec((TMB, PJ_BLOCK), lambda i, s: (i, pj(s)))]
    if split_in:
        h_specs.append(pl.BlockSpec((N_SAMPLE, PJ_BLOCK), lambda i, s: (0, pj(s))))
    if final:
        out_specs = [pl.BlockSpec((TMB, D_MODEL), row), pl.BlockSpec((N_SAMPLE, D_MODEL), const)]
        out_shape = [jax.ShapeDtypeStruct((N_PROMPT, D_MODEL), F32),
                     jax.ShapeDtypeStruct((N_SAMPLE, D_MODEL), F32)]
        scratch = [pltpu.VMEM((TMB, D_MODEL), BF16)]
    else:
        out_specs = [pl.BlockSpec((TMB, D_MODEL), row), pl.BlockSpec((TMB, D_MODEL), row)]
        out_shape = [jax.ShapeDtypeStruct((N_TOK, D_MODEL), F32),
                     jax.ShapeDtypeStruct((N_TOK, D_MODEL), BF16)]
        scratch = []
    return pl.pallas_call(
        functools.partial(_proj_ffn_kernel, split_in=split_in, final=final),
        grid=(N_TILES, N_PJ + N_FF),
        in_specs=[
            *h_specs,
            pl.BlockSpec((TMB, PJ_BLOCK), lambda i, s: (i, pj(s))),
            pl.BlockSpec((PJ_BLOCK, D_MODEL), lambda i, s: (pj(s), 0)),
            pl.BlockSpec((1, D_MODEL), const),
            pl.BlockSpec((None, D_MODEL, FF_BLOCK), lambda i, s: (layer, 0, ff(s))),
            pl.BlockSpec((None, FF_BLOCK, D_MODEL), lambda i, s: (layer, ff(s), 0)),
            pl.BlockSpec((1, D_MODEL), const),
        ],
        out_specs=out_specs,
        out_shape=out_shape,
        scratch_shapes=scratch,
        compiler_params=_params(("arbitrary", "arbitrary")),
        name="proj_ffn_final" if final else "proj_ffn",
    )(*(h if split_in else (h,)), y, w_out, g_ffn, w_up, w_down, g_next)


N_HG_PROMPT = N_PROMPT // TM
N_HG = N_HG_PROMPT + 1
CHUNKS_PER_TILE = TM // HG_CHUNK
SAMPLE_CHUNKS = N_SAMPLE // HG_CHUNK
ITEMS_PER_CHUNK = HG_CHUNK // DEC_SEQ


def _hgrn_scores(q, k, f, b_ref, r0, seg):
    n = HG_CHUNK
    sub = 8
    apart = [(lax.broadcasted_iota(jnp.int32, (sub, n), 0) + r * sub)
             ^ lax.broadcasted_iota(jnp.int32, (sub, n), 1) for r in range(n // sub)]
    scores = [None] * (n // sub)

    def merge(r, h, block):
        if scores[r] is None:
            scores[r] = block if 2 * h == n else jnp.where(apart[r] < 2 * h, block, 0.0)
        else:
            scores[r] = jnp.where(apart[r] < 2 * h, block, scores[r])

    def coarse(h):
        qs, ks = [], []
        zero = jnp.zeros((h, C_DIM), F32)
        for g in range(n // (2 * h)):
            base = r0 + g * 2 * h
            beta = b_ref[base + h - 1:base + h, :]
            lo = slice(g * 2 * h, g * 2 * h + h)
            up = slice(g * 2 * h + h, (g + 1) * 2 * h)
            ks += [k[lo] * jnp.exp2(beta - b_ref[base:base + h, :]), zero]
            qs.append(q[up] * jnp.exp2(b_ref[base + h:base + 2 * h, :] - beta))
        return jnp.concatenate(qs, axis=0).astype(BF16), jnp.concatenate(ks, axis=0).astype(BF16)

    def fine(h):
        if h == 4:
            qs, ks = [], []
            upper = lax.broadcasted_iota(jnp.int32, (8, C_DIM), 0) >= 4
            for g in range(n // 8):
                sl = slice(8 * g, 8 * g + 8)
                base = r0 + 8 * g
                e = jnp.exp2(-jnp.abs(b_ref[base:base + 8, :] - b_ref[base + 3:base + 4, :]))
                qs.append(jnp.where(upper, q[sl] * e, 0.0))
                ks.append(jnp.where(upper, 0.0, k[sl] * e))
            qt = jnp.concatenate(qs, axis=0)
            kt = jnp.concatenate(ks, axis=0)
        elif h == 2:
            r4 = lax.broadcasted_iota(jnp.int32, (n, C_DIM), 0) & 3
            f_next = pltpu.roll(f, n - 1, 0)
            f_prev = pltpu.roll(f, 1, 0)
            kt = jnp.where(r4 == 0, k * f_next, jnp.where(r4 == 1, k, 0.0))
            qt = jnp.where(r4 == 2, q * f, jnp.where(r4 == 3, q * f * f_prev, 0.0))
        else:
            odd = (lax.broadcasted_iota(jnp.int32, (n, C_DIM), 0) & 1) == 1
            kt = jnp.where(odd, 0.0, k)
            qt = jnp.where(odd, q * f, 0.0)
        return qt.astype(BF16), kt.astype(BF16)

    h = seg // 2
    while h >= 1:
        qt, kt = coarse(h) if h >= 8 else fine(h)
        s_h = lax.dot_general(qt, kt, _NT, preferred_element_type=F32)
        if h >= 8:
            per_group = h // sub
            for i in range(n // 2 // sub):
                g, j = divmod(i, per_group)
                merge((g * 2 * h + h) // sub + j, h, s_h[i * sub:(i + 1) * sub])
        else:
            for r in range(n // sub):
                merge(r, h, s_h[r * sub:(r + 1) * sub])
        h //= 2
    diag = jnp.sum(q * k, axis=1, keepdims=True)
    blocks = []
    for r in range(n // sub):
        lower = jnp.zeros((sub, n), F32) if scores[r] is None else scores[r]
        blocks.append(jnp.where(apart[r] == 0, diag[r * sub:(r + 1) * sub], lower))
    return jnp.concatenate(blocks, axis=0)


def _hgrn_kernel(xn0_ref, xn_next_ref, xn_s_ref, wq_ref, wf_ref, wi_ref, wg_ref, lb_ref, go_ref, trip_ref,
                 tris_ref, s0_ref, og_ref, sp_ref, ss_ref, w_ref, z_ref, f_ref, b_ref, st_ref):
    t = pl.program_id(1)

    lbx = lb_ref[...]
    e = jnp.exp(lbx - jnp.max(lbx, axis=0, keepdims=True))
    p = e / jnp.sum(e, axis=0, keepdims=True)
    lb = (p[0:1, :] + p[1:2, :]) - p[0:1, :]

    @pl.when(t == 0)
    def _():
        st_ref[...] = jnp.zeros_like(st_ref)
        for c, src in enumerate((wq_ref, wf_ref, wi_ref, wg_ref)):
            w_ref[:, c * C_DIM:(c + 1) * C_DIM] = src[...].astype(BF16)
        z_ref[...] = _dot(xn0_ref[...], w_ref[...])

    def cumulative_log_decay(tri_ref, n_chunks):
        lf = jnp.log2(f_ref[0:n_chunks * HG_CHUNK, :])
        lf = jnp.concatenate([lf[c * HG_CHUNK:(c + 1) * HG_CHUNK] for c in range(n_chunks)], axis=1)
        hi = lf.astype(BF16)
        lo = (lf - hi.astype(F32)).astype(BF16)
        bb = _dot(tri_ref[...], jnp.concatenate([hi, lo], axis=1))
        for c in range(n_chunks):
            lanes = slice(c * C_DIM, (c + 1) * C_DIM)
            b_ref[c * HG_CHUNK:(c + 1) * HG_CHUNK, :] = bb[:, lanes] + bb[:, n_chunks * C_DIM:][:, lanes]

    def chunk(c, seg):
        r0 = c * HG_CHUNK
        rows = slice(r0, r0 + HG_CHUNK)
        zq = z_ref[rows, 0:C_DIM]
        f = f_ref[rows, :]
        v = z_ref[rows, 2 * C_DIM:3 * C_DIM]
        gate = _sigmoid(z_ref[rows, 3 * C_DIM:4 * C_DIM])
        q = zq * _sigmoid(zq)
        k = 1.0 - f
        scores = _hgrn_scores(q, k, f, b_ref, r0, seg)
        o = _dot(scores.astype(BF16), v.astype(BF16))
        b = b_ref[rows, :]
        qe = (q * jnp.exp2(b)).astype(BF16)
        if seg == HG_CHUNK:
            s_t = st_ref[...]
            o = o + lax.dot_general(qe, s_t.astype(BF16), _NT, preferred_element_type=F32)
            b_last = b_ref[r0 + HG_CHUNK - 1:r0 + HG_CHUNK, :]
            ke = (k * jnp.exp2(b_last - b)).astype(BF16)
            st_ref[...] = s_t * jnp.exp2(b_last) + _dot(v.T.astype(BF16), ke)
        else:
            outs = []
            v_t = v.T.astype(BF16)
            row = lax.broadcasted_iota(jnp.int32, (HG_CHUNK, C_DIM), 0)
            for a in range(HG_CHUNK // seg):
                item = c * (HG_CHUNK // seg) + a
                r = slice(a * seg, (a + 1) * seg)
                s_t = s0_ref[item].T
                outs.append(lax.dot_general(qe[r], s_t.astype(BF16), _NT, preferred_element_type=F32))
                b_last = b_ref[r0 + (a + 1) * seg - 1:r0 + (a + 1) * seg, :]
                own = (row >= a * seg) & (row < (a + 1) * seg)
                ke = jnp.where(own, k * jnp.exp2(jnp.minimum(b_last - b, 0.0)), 0.0).astype(BF16)
                s_new = s_t * jnp.exp2(b_last) + _dot(v_t, ke)
                ss_ref[item] = s_new.T
            o = o + jnp.concatenate(outs, axis=0)
        y = _rms(o, go_ref[...]) * gate
        og_ref[rows, :] = y.astype(BF16)

    def tile(tri_ref, seg, n_chunks, project_next):
        rows = slice(0, n_chunks * HG_CHUNK)
        f_ref[rows, :] = lb + (1.0 - lb) * _sigmoid(z_ref[rows, C_DIM:2 * C_DIM])
        cumulative_log_decay(tri_ref, n_chunks)
        z_next = []
        for c in range(n_chunks):
            chunk(c, seg)
            if project_next:
                z_next.append(_dot(xn_next_ref[c * HG_CHUNK:(c + 1) * HG_CHUNK, :], w_ref[...]))
        if project_next:
            z_ref[...] = jnp.concatenate(z_next, axis=0)

    @pl.when(t < N_HG_PROMPT)
    def _():
        tile(trip_ref, HG_CHUNK, CHUNKS_PER_TILE, True)

    @pl.when(t == N_HG_PROMPT - 1)
    def _():
        sp_ref[...] = st_ref[...].T

    @pl.when(t == N_HG_PROMPT)
    def _():
        z_ref[0:N_SAMPLE, :] = _dot(xn_s_ref[...], w_ref[...])
        tile(tris_ref, DEC_SEQ, SAMPLE_CHUNKS, False)


def _hgrn(xn, w_in, lb, g_onorm, tri_p, tri_s, state0):
    c2 = lambda j, t: (0, 0)
    head_col = lambda j, t: (0, j)
    w_specs = [pl.BlockSpec((D_MODEL, C_DIM), functools.partial(lambda j, t, c: (0, c * C_HEADS + j), c=c))
               for c in range(4)]
    return pl.pallas_call(
        _hgrn_kernel,
        grid=(C_HEADS, N_HG),
        in_specs=[
            pl.BlockSpec((TM, D_MODEL), c2),
            pl.BlockSpec((TM, D_MODEL), lambda j, t: (jnp.minimum(t + 1, N_HG_PROMPT - 1), 0)),
            pl.BlockSpec((N_SAMPLE, D_MODEL), lambda j, t: (N_PROMPT // N_SAMPLE, 0)),
            *w_specs,
            pl.BlockSpec((2, C_DIM), head_col),
            pl.BlockSpec((1, C_DIM), head_col),
            pl.BlockSpec((HG_CHUNK, HG_CHUNK), c2),
            pl.BlockSpec((HG_CHUNK, HG_CHUNK), c2),
            pl.BlockSpec((DEC_BATCH, None, C_DIM, C_DIM), lambda j, t: (0, j, 0, 0)),
        ],
        out_specs=[
            pl.BlockSpec((TM, C_DIM), lambda j, t: (t, j)),
            pl.BlockSpec((None, C_DIM, C_DIM), lambda j, t: (j, 0, 0)),
            pl.BlockSpec((DEC_BATCH, None, C_DIM, C_DIM), lambda j, t: (0, j, 0, 0)),
        ],
        out_shape=[
            jax.ShapeDtypeStruct((N_TOK, C_HEADS * C_DIM), BF16),
            jax.ShapeDtypeStruct((C_HEADS, C_DIM, C_DIM), F32),
            jax.ShapeDtypeStruct((DEC_BATCH, C_HEADS, C_DIM, C_DIM), F32),
        ],
        scratch_shapes=[
            pltpu.VMEM((D_MODEL, 4 * C_DIM), BF16),
            pltpu.VMEM((TM, 4 * C_DIM), F32),
            pltpu.VMEM((TM, C_DIM), F32),
            pltpu.VMEM((TM, C_DIM), F32),
            pltpu.VMEM((C_DIM, C_DIM), F32),
        ],
        compiler_params=_params(("arbitrary", "arbitrary")),
        name="hgrn",
    )(xn, xn, xn, w_in, w_in, w_in, w_in, lb, g_onorm, tri_p, tri_s, state0)


def _rope_tables():
    half = ROT_DIM // 2
    inv = ROPE_THETA ** (-(jnp.arange(half, dtype=F32) * (2.0 / ROT_DIM)))
    pos = jnp.concatenate([jnp.arange(N_PROMPT, dtype=jnp.int32),
                           jnp.tile(PAST_LEN + jnp.arange(DEC_SEQ, dtype=jnp.int32), DEC_BATCH)])
    ang = pos.astype(F32)[:, None] * inv[None, :]
    cos, sin = jnp.cos(ang), jnp.sin(ang)
    ones = jnp.ones((N_TOK, HEAD_DIM - ROT_DIM), F32)
    ca = jnp.concatenate([cos, cos, ones], axis=1)
    cb = jnp.concatenate([-sin, sin, 0.0 * ones], axis=1)
    return jnp.tile(ca, (1, LANES // HEAD_DIM)), jnp.tile(cb, (1, LANES // HEAD_DIM))


def kernel(x_prompt, x_sample, cache_swa_k, cache_swa_v, state_hgrn, norm_mix, norm_ffn, norm_final,
           w_in_even, w_out_even, attn_sinks, gmlp_vnorm, gmlp_ws, gmlp_bias, w_in_odd, w_out_odd,
           hgrn_lb, hgrn_onorm, w_ffn_up, w_ffn_down):
    xp = x_prompt.reshape(N_PROMPT, D_MODEL)
    xs = x_sample.reshape(N_SAMPLE, D_MODEL)
    ca, cb = _rope_tables()

    q, k, v, u, gv = _even_in(xp, xs, norm_mix[0:1], w_in_even[0], ca, cb, gmlp_vnorm[0:1])
    ws = gmlp_ws[0]
    cidx = jnp.arange(MIX_TM) // CHUNK
    w_p = jnp.where(cidx[:, None] >= cidx[None, :], ws, 0.0).astype(BF16)
    eye = jnp.eye(ITEMS_PER_TILE, dtype=F32)
    w_s = jnp.einsum('ab,gpq->gapbq', eye, ws[:, :DEC_SEQ, :DEC_SEQ]).reshape(B_GROUPS, MIX_TM, MIX_TM).astype(BF16)
    b_p = gmlp_bias[0].T
    b_s = jnp.tile(gmlp_bias[0][:, :DEC_SEQ], (1, ITEMS_PER_TILE)).T
    ab, k_s, v_s = _mixer(attn_sinks[0], q, k, v,
                          cache_swa_k[0].reshape(DEC_BATCH, WINDOW, KV_WIDTH),
                          cache_swa_v[0].reshape(DEC_BATCH, WINDOW, KV_WIDTH),
                          u, gv, w_p, w_s, b_p, b_s)
    h1, xn1 = _proj_ffn((xp, xs), ab, w_out_even[0], norm_ffn[0:1], w_ffn_up, w_ffn_down, 0,
                        norm_mix[1:2], final=False)

    tri_p = jnp.tril(jnp.ones((HG_CHUNK, HG_CHUNK), F32))
    seg = jnp.arange(HG_CHUNK) // DEC_SEQ
    tri_s = jnp.where(seg[:, None] == seg[None, :], tri_p, 0.0)
    og, s_p, s_s = _hgrn(xn1, w_in_odd[0], hgrn_lb, hgrn_onorm[0:1], tri_p.astype(BF16), tri_s.astype(BF16),
                         state_hgrn[0])
    y_p, y_s = _proj_ffn(h1, og, w_out_odd[0], norm_ffn[1:2], w_ffn_up, w_ffn_down, 1,
                         norm_final[None, :], final=True)

    kv_shape = (1, 1, WINDOW, A_KV_HEADS, HEAD_DIM)
    kv_s_shape = (1, DEC_BATCH, WINDOW, A_KV_HEADS, HEAD_DIM)
    return (y_p.reshape(1, N_PROMPT, D_MODEL),
            y_s.reshape(DEC_BATCH, DEC_SEQ, D_MODEL),
            k[N_PROMPT - WINDOW:N_PROMPT].reshape(kv_shape),
            v[N_PROMPT - WINDOW:N_PROMPT].reshape(kv_shape),
            k_s.reshape(kv_s_shape),
            v_s.reshape(kv_s_shape),
            gv[N_PROMPT:].reshape(1, DEC_BATCH, DEC_SEQ, B_GROUPS, GMLP_DIM),
            s_p[None, None],
            s_s[None])
```

```python
import functools
import math

import jax
import jax.numpy as jnp
from jax import lax
from jax.experimental import pallas as pl
from jax.experimental.pallas import tpu as pltpu

F32 = jnp.float32
BF16 = jnp.bfloat16

D_MODEL = 2048
N_PROMPT = 8192
DEC_BATCH = 16
DEC_SEQ = 32
N_SAMPLE = DEC_BATCH * DEC_SEQ
N_TOK = N_PROMPT + N_SAMPLE
PAST_LEN = 2048
EPS = 1e-6

HEAD_DIM = 64
A_HEADS = 16
A_KV_HEADS = 4
A_WIDTH = A_HEADS * HEAD_DIM
KV_WIDTH = A_KV_HEADS * HEAD_DIM
ROT_DIM = 16
ROPE_THETA = 500000.0
WINDOW = 128
CHUNK = 64
B_GROUPS = 8
GMLP_DIM = 128
B_WIDTH = B_GROUPS * GMLP_DIM
EVEN_IN = A_WIDTH + 2 * KV_WIDTH + 2 * B_WIDTH
C_HEADS = 16
C_DIM = 128
D_FF = 4 * D_MODEL

LANES = 128
TM = 1024
TMB = 1088
ROW_CHUNK = 272
N_TILES = N_TOK // TMB
TAIL = N_PROMPT - (N_TILES - 1) * TMB
MIX_TM = 128
HG_CHUNK = 128
IN_BLOCK = 512
PJ_BLOCK = 256
FF_BLOCK = 512
MXU_N = 256
VMEM_LIMIT = 58 * 1024 * 1024

_NT = (((1,), (1,)), ((), ()))


def _params(sem):
    return pltpu.CompilerParams(dimension_semantics=sem, vmem_limit_bytes=VMEM_LIMIT)


def _rms(x, g):
    return x * lax.rsqrt(jnp.mean(x * x, axis=-1, keepdims=True) + EPS) * g


def _sigmoid(x):
    return 0.5 * jnp.tanh(0.5 * x) + 0.5


def _gelu(x):
    c = math.sqrt(2.0 / math.pi)
    return x * (0.5 * (1.0 + jnp.tanh(c * (x + 0.044715 * (x * x * x)))))


def _dot(a, b):
    return jnp.dot(a, b, preferred_element_type=F32)


def _row_spans(start, stop, step=ROW_CHUNK):
    return [(r, min(r + step, stop)) for r in range(start, stop, step)]


def _even_in_kernel(xp_ref, xs_ref, g_ref, w_ref, ca_ref, cb_ref, gv_g_ref,
                    q_ref, k_ref, v_ref, u_ref, gv_ref, xn_ref, z_ref):
    i = pl.program_id(0)
    j = pl.program_id(1)

    def norm_rows(src_ref, src0, dst0, n):
        for a, b in _row_spans(0, n):
            xn_ref[dst0 + a:dst0 + b, :] = _rms(src_ref[src0 + a:src0 + b, :], g_ref[...]).astype(BF16)

    @pl.when((j == 0) & (i < N_TILES - 1))
    def _():
        norm_rows(xp_ref, 0, 0, TMB)

    @pl.when((j == 0) & (i == N_TILES - 1))
    def _():
        norm_rows(xp_ref, 0, 0, TAIL)
        norm_rows(xs_ref, 0, TAIL, N_SAMPLE)

    def rope(x):
        lane = lax.broadcasted_iota(jnp.int32, (TMB, LANES), 1)
        first = (lane & (HEAD_DIM - 1)) < (ROT_DIM // 2)
        partner = jnp.where(first, pltpu.roll(x, LANES - ROT_DIM // 2, 1), pltpu.roll(x, ROT_DIM // 2, 1))
        return x * ca_ref[...] + partner * cb_ref[...]

    tiles = [slice(t * LANES, (t + 1) * LANES) for t in range(IN_BLOCK // LANES)]

    def epilogue(c):
        if c < IN_Q:
            for sl in tiles:
                q_ref[:, sl] = (rope(z_ref[:, sl]) * (HEAD_DIM ** -0.5)).astype(BF16)
        elif c == IN_Q:
            for sl in tiles[:KV_WIDTH // LANES]:
                k_ref[:, sl] = rope(z_ref[:, sl])
            v_ref[...] = z_ref[:, KV_WIDTH:]
        elif c < IN_V:
            u_ref[...] = _gelu(z_ref[...]).astype(BF16)
        else:
            for sl in tiles:
                x = _gelu(z_ref[:, sl])
                xc = x - jnp.mean(x, axis=-1, keepdims=True)
                y = xc * lax.rsqrt(jnp.mean(xc * xc, axis=-1, keepdims=True) + EPS)
                gv_ref[:, sl] = y * gv_g_ref[:, sl]

    for c in range(IN_BLOCKS + 1):
        @pl.when(j == c)
        def _():
            if c < IN_BLOCKS:
                z = _dot(xn_ref[...], w_ref[...].astype(BF16))
            if c > 0:
                epilogue(c - 1)
            if c < IN_BLOCKS:
                z_ref[...] = z


IN_BLOCKS = EVEN_IN // IN_BLOCK
IN_Q = A_WIDTH // IN_BLOCK
IN_U = IN_Q + 1
IN_V = IN_U + B_WIDTH // IN_BLOCK


def _even_in(x_prompt, x_sample, g, w, ca, cb, gv_g):
    row = lambda i, j: (i, 0)
    const = lambda i, j: (0, 0)
    ahead = lambda i, j: (jnp.minimum(i + (j > IN_Q + 1).astype(jnp.int32), N_TILES - 1), 0)
    done = lambda j, first, count: jnp.clip(j - 1 - first, 0, count - 1)
    return pl.pallas_call(
        _even_in_kernel,
        grid=(N_TILES, IN_BLOCKS + 1),
        in_specs=[
            pl.BlockSpec((TMB, D_MODEL), ahead),
            pl.BlockSpec((N_SAMPLE, D_MODEL), const),
            pl.BlockSpec((1, D_MODEL), const),
            pl.BlockSpec((D_MODEL, IN_BLOCK), lambda i, j: (0, jnp.minimum(j, IN_BLOCKS - 1))),
            pl.BlockSpec((TMB, LANES), ahead),
            pl.BlockSpec((TMB, LANES), ahead),
            pl.BlockSpec((1, IN_BLOCK), lambda i, j: (0, done(j, IN_V, IN_BLOCKS - IN_V))),
        ],
        out_specs=[
            pl.BlockSpec((TMB, IN_BLOCK), lambda i, j: (i, done(j, 0, IN_Q))),
            pl.BlockSpec((TMB, KV_WIDTH), row),
            pl.BlockSpec((TMB, KV_WIDTH), row),
            pl.BlockSpec((TMB, IN_BLOCK), lambda i, j: (i, done(j, IN_U, IN_V - IN_U))),
            pl.BlockSpec((TMB, IN_BLOCK), lambda i, j: (i, done(j, IN_V, IN_BLOCKS - IN_V))),
        ],
        out_shape=[
            jax.ShapeDtypeStruct((N_TOK, A_WIDTH), BF16),
            jax.ShapeDtypeStruct((N_TOK, KV_WIDTH), F32),
            jax.ShapeDtypeStruct((N_TOK, KV_WIDTH), F32),
            jax.ShapeDtypeStruct((N_TOK, B_WIDTH), BF16),
            jax.ShapeDtypeStruct((N_TOK, B_WIDTH), F32),
        ],
        scratch_shapes=[pltpu.VMEM((TMB, D_MODEL), BF16), pltpu.VMEM((TMB, IN_BLOCK), F32)],
        compiler_params=_params(("arbitrary", "arbitrary")),
        name="even_in",
    )(x_prompt, x_sample, g, w, ca, cb, gv_g)


N_MIX_PROMPT = N_PROMPT // MIX_TM
ITEMS_PER_TILE = MIX_TM // DEC_SEQ
N_MIX = N_TOK // MIX_TM


def _spread_heads(x, kv_head):
    t = x[:, (kv_head // 2) * LANES:(kv_head // 2 + 1) * LANES]
    lane = lax.broadcasted_iota(jnp.int32, t.shape, 1)
    if kv_head % 2 == 0:
        lo = jnp.where(lane < HEAD_DIM, t, 0.0)
        hi = pltpu.roll(lo, HEAD_DIM, 1)
    else:
        hi = jnp.where(lane >= HEAD_DIM, t, 0.0)
        lo = pltpu.roll(hi, HEAD_DIM, 1)
    return lo, hi


def _kv_operands(kk, vv):
    s_len = kk[0].shape[0]
    kx = jnp.concatenate(kk, axis=0).astype(BF16)
    vx = jnp.concatenate(vv, axis=0)
    row = lax.broadcasted_iota(jnp.int32, vx.shape, 0)
    lane = lax.broadcasted_iota(jnp.int32, vx.shape, 1)
    own = ((row < s_len) & (lane < HEAD_DIM)) | ((row >= s_len) & (lane >= HEAD_DIM))
    ones = jnp.where(own, 1.0, 0.0)
    return kx, jnp.concatenate([vx, ones], axis=1).astype(BF16)


def _attend(q2, kx, vx1, sink_a, sink_b, allowed):
    s_len = kx.shape[0] // 2
    s = lax.dot_general(q2, kx, _NT, preferred_element_type=F32)
    ps, sink_terms = [], []
    for half, sink in ((0, sink_a), (1, sink_b)):
        sh = s[:, half * s_len:(half + 1) * s_len]
        if allowed is not None:
            sh = jnp.where(allowed, sh, -jnp.inf)
        m = jnp.maximum(jnp.max(sh, axis=-1, keepdims=True), sink)
        ps.append(jnp.exp(sh - m))
        sink_terms.append(jnp.exp(sink - m))
    o = _dot(jnp.concatenate(ps, axis=1).astype(BF16), vx1)
    lane = lax.broadcasted_iota(jnp.int32, (o.shape[0], LANES), 1)
    return o[:, :LANES] / (o[:, LANES:] + jnp.where(lane < HEAD_DIM, sink_terms[0], sink_terms[1]))


def _mixer_kernel(sink_ref, q_ref, kp_ref, kc_ref, vp_ref, vc_ref, ck_ref, cv_ref,
                  u_ref, gv_ref, wp_ref, ws_ref, bp_ref, bs_ref,
                  ab_ref, ko_ref, vo_ref):
    i = pl.program_id(0)

    def gmlp(w_ref, b_ref):
        for g in range(B_GROUPS):
            sl = slice(g * GMLP_DIM, (g + 1) * GMLP_DIM)
            mixed = _dot(w_ref[g], gv_ref[:, sl].astype(BF16)) + b_ref[:, g:g + 1]
            ab_ref[:, A_WIDTH + g * GMLP_DIM:A_WIDTH + (g + 1) * GMLP_DIM] = (
                u_ref[:, sl].astype(F32) * mixed).astype(BF16)

    @pl.when(i < N_MIX_PROMPT)
    def _prompt():
        kk = jnp.concatenate([kp_ref[...], kc_ref[...]], axis=0)
        vv = jnp.concatenate([vp_ref[...], vc_ref[...]], axis=0)
        row = lax.broadcasted_iota(jnp.int32, (MIX_TM, 2 * MIX_TM), 0)
        col = lax.broadcasted_iota(jnp.int32, (MIX_TM, 2 * MIX_TM), 1)
        allowed = ((row < CHUNK) & (col < 3 * CHUNK)) | ((row >= CHUNK) & (col >= CHUNK))
        allowed = allowed & ((col >= MIX_TM) | (i > 0))
        for j in range(A_KV_HEADS):
            kx, vx1 = _kv_operands(_spread_heads(kk, j), _spread_heads(vv, j))
            for p in (2 * j, 2 * j + 1):
                sl = slice(p * LANES, (p + 1) * LANES)
                o = _attend(q_ref[:, sl], kx, vx1, sink_ref[2 * p], sink_ref[2 * p + 1], allowed)
                ab_ref[:, sl] = o.astype(BF16)
        gmlp(wp_ref, bp_ref)

    @pl.when(i >= N_MIX_PROMPT)
    def _sample():
        span = WINDOW + DEC_SEQ
        k_parts, v_parts = [], []
        for a in range(ITEMS_PER_TILE):
            rows = slice(a * DEC_SEQ, (a + 1) * DEC_SEQ)
            k_parts += [ck_ref[a], kc_ref[rows, :]]
            v_parts += [cv_ref[a], vc_ref[rows, :]]
            ko_ref[a] = jnp.concatenate([ck_ref[a, DEC_SEQ:, :], kc_ref[rows, :]], axis=0)
            vo_ref[a] = jnp.concatenate([cv_ref[a, DEC_SEQ:, :], vc_ref[rows, :]], axis=0)
        kk = jnp.concatenate(k_parts, axis=0)
        vv = jnp.concatenate(v_parts, axis=0)
        row = lax.broadcasted_iota(jnp.int32, (MIX_TM, ITEMS_PER_TILE * span), 0)
        col = lax.broadcasted_iota(jnp.int32, (MIX_TM, ITEMS_PER_TILE * span), 1)
        allowed = None
        for a in range(ITEMS_PER_TILE):
            own = ((row >= a * DEC_SEQ) & (row < (a + 1) * DEC_SEQ)
                   & (col >= a * span) & (col < (a + 1) * span))
            allowed = own if allowed is None else allowed | own
        for j in range(A_KV_HEADS):
            kx, vx1 = _kv_operands(_spread_heads(kk, j), _spread_heads(vv, j))
            for p in (2 * j, 2 * j + 1):
                sl = slice(p * LANES, (p + 1) * LANES)
                o = _attend(q_ref[:, sl], kx, vx1, sink_ref[2 * p], sink_ref[2 * p + 1], allowed)
                ab_ref[:, sl] = o.astype(BF16)
        gmlp(ws_ref, bs_ref)


def _mixer(sinks, q, k, v, cache_k, cache_v, u, gv, w_p, w_s, b_p, b_s):
    row = lambda i: (i, 0)
    prev = lambda i: (jnp.maximum(i - 1, 0), 0)
    item = lambda i: (jnp.maximum(i - N_MIX_PROMPT, 0), 0, 0)
    c2 = lambda i: (0, 0)
    c3 = lambda i: (0, 0, 0)
    cache_block = (ITEMS_PER_TILE, WINDOW, KV_WIDTH)
    return pl.pallas_call(
        _mixer_kernel,
        grid=(N_MIX,),
        in_specs=[
            pl.BlockSpec(memory_space=pltpu.SMEM),
            pl.BlockSpec((MIX_TM, A_WIDTH), row),
            pl.BlockSpec((MIX_TM, KV_WIDTH), prev),
            pl.BlockSpec((MIX_TM, KV_WIDTH), row),
            pl.BlockSpec((MIX_TM, KV_WIDTH), prev),
            pl.BlockSpec((MIX_TM, KV_WIDTH), row),
            pl.BlockSpec(cache_block, item),
            pl.BlockSpec(cache_block, item),
            pl.BlockSpec((MIX_TM, B_WIDTH), row),
            pl.BlockSpec((MIX_TM, B_WIDTH), row),
            pl.BlockSpec((B_GROUPS, MIX_TM, MIX_TM), c3),
            pl.BlockSpec((B_GROUPS, MIX_TM, MIX_TM), c3),
            pl.BlockSpec((MIX_TM, B_GROUPS), c2),
            pl.BlockSpec((MIX_TM, B_GROUPS), c2),
        ],
        out_specs=[
            pl.BlockSpec((MIX_TM, A_WIDTH + B_WIDTH), row),
            pl.BlockSpec(cache_block, item),
            pl.BlockSpec(cache_block, item),
        ],
        out_shape=[
            jax.ShapeDtypeStruct((N_TOK, A_WIDTH + B_WIDTH), BF16),
            jax.ShapeDtypeStruct((DEC_BATCH, WINDOW, KV_WIDTH), F32),
            jax.ShapeDtypeStruct((DEC_BATCH, WINDOW, KV_WIDTH), F32),
        ],
        compiler_params=_params(("arbitrary",)),
        name="mixer",
    )(sinks, q, k, k, v, v, cache_k, cache_v, u, gv, w_p, w_s, b_p, b_s)


N_PJ = D_MODEL // PJ_BLOCK
N_FF = D_FF // FF_BLOCK


def _proj_ffn_kernel(*refs, split_in, final):
    refs = list(refs)
    h_refs = [refs.pop(0) for _ in range(2 if split_in else 1)]
    y_ref, wo_ref, gf_ref, wu_ref, wd_ref, gn_ref, o_ref = refs[:7]
    if final:
        ys_ref, xn_ref = refs[7:]
    else:
        (xn_ref,) = refs[7:]
    i = pl.program_id(0)
    s = pl.program_id(1)
    last_tile = i == N_TILES - 1
    cols = [slice(c * MXU_N, (c + 1) * MXU_N) for c in range(D_MODEL // MXU_N)]

    def accumulate(lhs, w, first=False):
        for sl in cols:
            if first:
                o_ref[:, sl] = _dot(lhs, w[:, sl])
            else:
                o_ref[:, sl] += _dot(lhs, w[:, sl])

    @pl.when(s == 0)
    def _():
        accumulate(y_ref[...], wo_ref[...].astype(BF16), first=True)

    @pl.when((s > 0) & (s < N_PJ))
    def _():
        accumulate(y_ref[...], wo_ref[...].astype(BF16))

    for c in range(N_PJ):
        sl = slice(c * PJ_BLOCK, (c + 1) * PJ_BLOCK)
        if split_in:
            hp_ref, hs_ref = h_refs

            @pl.when((s == c) & jnp.logical_not(last_tile))
            def _():
                o_ref[:, sl] += hp_ref[...]

            @pl.when((s == c) & last_tile)
            def _():
                o_ref[0:TAIL, sl] += hp_ref[0:TAIL, :]
                o_ref[TAIL:, sl] += hs_ref[...]
        else:
            @pl.when(s == c)
            def _():
                o_ref[:, sl] += h_refs[0][...]

    @pl.when(s == N_PJ)
    def _():
        for a, b in _row_spans(0, TMB):
            xn_ref[a:b, :] = _rms(o_ref[a:b, :], gf_ref[...]).astype(BF16)

    @pl.when(s >= N_PJ)
    def _():
        t = jnp.maximum(_dot(xn_ref[...], wu_ref[...].astype(BF16)), 0.0)
        accumulate((t * t).astype(BF16), wd_ref[...].astype(BF16))

    last_step = s == N_PJ + N_FF - 1
    if final:
        @pl.when(last_step & jnp.logical_not(last_tile))
        def _():
            for a, b in _row_spans(0, TMB):
                o_ref[a:b, :] = _rms(o_ref[a:b, :], gn_ref[...])

        @pl.when(last_step & last_tile)
        def _():
            for a, b in _row_spans(0, TAIL):
                o_ref[a:b, :] = _rms(o_ref[a:b, :], gn_ref[...])
            for a, b in _row_spans(0, N_SAMPLE):
                ys_ref[a:b, :] = _rms(o_ref[TAIL + a:TAIL + b, :], gn_ref[...])
    else:
        @pl.when(last_step)
        def _():
            for a, b in _row_spans(0, TMB):
                xn_ref[a:b, :] = _rms(o_ref[a:b, :], gn_ref[...]).astype(BF16)


def _proj_ffn(h, y, w_out, g_ffn, w_up, w_down, layer, g_next, final):
    split_in = isinstance(h, tuple)
    row = lambda i, s: (i, 0)
    const = lambda i, s: (0, 0)
    pj = lambda s: jnp.minimum(s, N_PJ - 1)
    ff = lambda s: jnp.maximum(s - N_PJ, 0)
    h_specs = [pl.BlockSpec((TMB, PJ_BLOCK), lambda i, s: (i, pj(s)))]
    if split_in:
        h_specs.append(pl.BlockSpec((N_SAMPLE, PJ_BLOCK), lambda i, s: (0, pj(s))))
    if final:
        out_specs = [pl.BlockSpec((TMB, D_MODEL), row), pl.BlockSpec((N_SAMPLE, D_MODEL), const)]
        out_shape = [jax.ShapeDtypeStruct((N_PROMPT, D_MODEL), F32),
                     jax.ShapeDtypeStruct((N_SAMPLE, D_MODEL), F32)]
        scratch = [pltpu.VMEM((TMB, D_MODEL), BF16)]
    else:
        out_specs = [pl.BlockSpec((TMB, D_MODEL), row), pl.BlockSpec((TMB, D_MODEL), row)]
        out_shape = [jax.ShapeDtypeStruct((N_TOK, D_MODEL), F32),
                     jax.ShapeDtypeStruct((N_TOK, D_MODEL), BF16)]
        scratch = []
    return pl.pallas_call(
        functools.partial(_proj_ffn_kernel, split_in=split_in, final=final),
        grid=(N_TILES, N_PJ + N_FF),
        in_specs=[
            *h_specs,
            pl.BlockSpec((TMB, PJ_BLOCK), lambda i, s: (i, pj(s))),
            pl.BlockSpec((PJ_BLOCK, D_MODEL), lambda i, s: (pj(s), 0)),
            pl.BlockSpec((1, D_MODEL), const),
            pl.BlockSpec((None, D_MODEL, FF_BLOCK), lambda i, s: (layer, 0, ff(s))),
            pl.BlockSpec((None, FF_BLOCK, D_MODEL), lambda i, s: (layer, ff(s), 0)),
            pl.BlockSpec((1, D_MODEL), const),
        ],
        out_specs=out_specs,
        out_shape=out_shape,
        scratch_shapes=scratch,
        compiler_params=_params(("arbitrary", "arbitrary")),
        name="proj_ffn_final" if final else "proj_ffn",
    )(*(h if split_in else (h,)), y, w_out, g_ffn, w_up, w_down, g_next)


N_HG_PROMPT = N_PROMPT // TM
N_HG = N_HG_PROMPT + 1
CHUNKS_PER_TILE = TM // HG_CHUNK
SAMPLE_CHUNKS = N_SAMPLE // HG_CHUNK
ITEMS_PER_CHUNK = HG_CHUNK // DEC_SEQ


def _hgrn_scores(q, k, f, b_ref, r0, seg):
    n = HG_CHUNK
    sub = 8
    apart = [(lax.broadcasted_iota(jnp.int32, (sub, n), 0) + r * sub)
             ^ lax.broadcasted_iota(jnp.int32, (sub, n), 1) for r in range(n // sub)]
    scores = [None] * (n // sub)

    def merge(r, h, block):
        if scores[r] is None:
            scores[r] = block if 2 * h == n else jnp.where(apart[r] < 2 * h, block, 0.0)
        else:
            scores[r] = jnp.where(apart[r] < 2 * h, block, scores[r])

    def coarse(h):
        qs, ks = [], []
        zero = jnp.zeros((h, C_DIM), F32)
        for g in range(n // (2 * h)):
            base = r0 + g * 2 * h
            beta = b_ref[base + h - 1:base + h, :]
            lo = slice(g * 2 * h, g * 2 * h + h)
            up = slice(g * 2 * h + h, (g + 1) * 2 * h)
            ks += [k[lo] * jnp.exp2(beta - b_ref[base:base + h, :]), zero]
            qs.append(q[up] * jnp.exp2(b_ref[base + h:base + 2 * h, :] - beta))
        return jnp.concatenate(qs, axis=0).astype(BF16), jnp.concatenate(ks, axis=0).astype(BF16)

    def fine(h):
        if h == 4:
            qs, ks = [], []
            upper = lax.broadcasted_iota(jnp.int32, (8, C_DIM), 0) >= 4
            for g in range(n // 8):
                sl = slice(8 * g, 8 * g + 8)
                base = r0 + 8 * g
                e = jnp.exp2(-jnp.abs(b_ref[base:base + 8, :] - b_ref[base + 3:base + 4, :]))
                qs.append(jnp.where(upper, q[sl] * e, 0.0))
                ks.append(jnp.where(upper, 0.0, k[sl] * e))
            qt = jnp.concatenate(qs, axis=0)
            kt = jnp.concatenate(ks, axis=0)
        elif h == 2:
            r4 = lax.broadcasted_iota(jnp.int32, (n, C_DIM), 0) & 3
            f_next = pltpu.roll(f, n - 1, 0)
            f_prev = pltpu.roll(f, 1, 0)
            kt = jnp.where(r4 == 0, k * f_next, jnp.where(r4 == 1, k, 0.0))
            qt = jnp.where(r4 == 2, q * f, jnp.where(r4 == 3, q * f * f_prev, 0.0))
        else:
            odd = (lax.broadcasted_iota(jnp.int32, (n, C_DIM), 0) & 1) == 1
            kt = jnp.where(odd, 0.0, k)
            qt = jnp.where(odd, q * f, 0.0)
        return qt.astype(BF16), kt.astype(BF16)

    h = seg // 2
    while h >= 1:
        qt, kt = coarse(h) if h >= 8 else fine(h)
        s_h = lax.dot_general(qt, kt, _NT, preferred_element_type=F32)
        if h >= 8:
            per_group = h // sub
            for i in range(n // 2 // sub):
                g, j = divmod(i, per_group)
                merge((g * 2 * h + h) // sub + j, h, s_h[i * sub:(i + 1) * sub])
        else:
            for r in range(n // sub):
                merge(r, h, s_h[r * sub:(r + 1) * sub])
        h //= 2
    diag = jnp.sum(q * k, axis=1, keepdims=True)
    blocks = []
    for r in range(n // sub):
        lower = jnp.zeros((sub, n), F32) if scores[r] is None else scores[r]
        blocks.append(jnp.where(apart[r] == 0, diag[r * sub:(r + 1) * sub], lower))
    return jnp.concatenate(blocks, axis=0)


def _hgrn_kernel(xn0_ref, xn_next_ref, xn_s_ref, wq_ref, wf_ref, wi_ref, wg_ref, lb_ref, go_ref, trip_ref,
                 tris_ref, s0_ref, og_ref, sp_ref, ss_ref, w_ref, z_ref, f_ref, b_ref, st_ref):
    t = pl.program_id(1)

    lbx = lb_ref[...]
    e = jnp.exp(lbx - jnp.max(lbx, axis=0, keepdims=True))
    p = e / jnp.sum(e, axis=0, keepdims=True)
    lb = (p[0:1, :] + p[1:2, :]) - p[0:1, :]

    @pl.when(t == 0)
    def _():
        st_ref[...] = jnp.zeros_like(st_ref)
        for c, src in enumerate((wq_ref, wf_ref, wi_ref, wg_ref)):
            w_ref[:, c * C_DIM:(c + 1) * C_DIM] = src[...].astype(BF16)
        z_ref[...] = _dot(xn0_ref[...], w_ref[...])

    def cumulative_log_decay(tri_ref, n_chunks):
        lf = jnp.log2(f_ref[0:n_chunks * HG_CHUNK, :])
        lf = jnp.concatenate([lf[c * HG_CHUNK:(c + 1) * HG_CHUNK] for c in range(n_chunks)], axis=1)
        hi = lf.astype(BF16)
        lo = (lf - hi.astype(F32)).astype(BF16)
        bb = _dot(tri_ref[...], jnp.concatenate([hi, lo], axis=1))
        for c in range(n_chunks):
            lanes = slice(c * C_DIM, (c + 1) * C_DIM)
            b_ref[c * HG_CHUNK:(c + 1) * HG_CHUNK, :] = bb[:, lanes] + bb[:, n_chunks * C_DIM:][:, lanes]

    def chunk(c, seg):
        r0 = c * HG_CHUNK
        rows = slice(r0, r0 + HG_CHUNK)
        zq = z_ref[rows, 0:C_DIM]
        f = f_ref[rows, :]
        v = z_ref[rows, 2 * C_DIM:3 * C_DIM]
        gate = _sigmoid(z_ref[rows, 3 * C_DIM:4 * C_DIM])
        q = zq * _sigmoid(zq)
        k = 1.0 - f
        scores = _hgrn_scores(q, k, f, b_ref, r0, seg)
        o = _dot(scores.astype(BF16), v.astype(BF16))
        b = b_ref[rows, :]
        qe = (q * jnp.exp2(b)).astype(BF16)
        if seg == HG_CHUNK:
            s_t = st_ref[...]
            o = o + lax.dot_general(qe, s_t.astype(BF16), _NT, preferred_element_type=F32)
            b_last = b_ref[r0 + HG_CHUNK - 1:r0 + HG_CHUNK, :]
            ke = (k * jnp.exp2(b_last - b)).astype(BF16)
            st_ref[...] = s_t * jnp.exp2(b_last) + _dot(v.T.astype(BF16), ke)
        else:
            outs = []
            v_t = v.T.astype(BF16)
            row = lax.broadcasted_iota(jnp.int32, (HG_CHUNK, C_DIM), 0)
            for a in range(HG_CHUNK // seg):
                item = c * (HG_CHUNK // seg) + a
                r = slice(a * seg, (a + 1) * seg)
                s_t = s0_ref[item].T
                outs.append(lax.dot_general(qe[r], s_t.astype(BF16), _NT, preferred_element_type=F32))
                b_last = b_ref[r0 + (a + 1) * seg - 1:r0 + (a + 1) * seg, :]
                own = (row >= a * seg) & (row < (a + 1) * seg)
                ke = jnp.where(own, k * jnp.exp2(jnp.minimum(b_last - b, 0.0)), 0.0).astype(BF16)
                s_new = s_t * jnp.exp2(b_last) + _dot(v_t, ke)
                ss_ref[item] = s_new.T
            o = o + jnp.concatenate(outs, axis=0)
        y = _rms(o, go_ref[...]) * gate
        og_ref[rows, :] = y.astype(BF16)

    def tile(tri_ref, seg, n_chunks, project_next):
        rows = slice(0, n_chunks * HG_CHUNK)
        f_ref[rows, :] = lb + (1.0 - lb) * _sigmoid(z_ref[rows, C_DIM:2 * C_DIM])
        cumulative_log_decay(tri_ref, n_chunks)
        z_next = []
        for c in range(n_chunks):
            chunk(c, seg)
            if project_next:
                z_next.append(_dot(xn_next_ref[c * HG_CHUNK:(c + 1) * HG_CHUNK, :], w_ref[...]))
        if project_next:
            z_ref[...] = jnp.concatenate(z_next, axis=0)

    @pl.when(t < N_HG_PROMPT)
    def _():
        tile(trip_ref, HG_CHUNK, CHUNKS_PER_TILE, True)

    @pl.when(t == N_HG_PROMPT - 1)
    def _():
        sp_ref[...] = st_ref[...].T

    @pl.when(t == N_HG_PROMPT)
    def _():
        z_ref[0:N_SAMPLE, :] = _dot(xn_s_ref[...], w_ref[...])
        tile(tris_ref, DEC_SEQ, SAMPLE_CHUNKS, False)


def _hgrn(xn, w_in, lb, g_onorm, tri_p, tri_s, state0):
    c2 = lambda j, t: (0, 0)
    head_col = lambda j, t: (0, j)
    w_specs = [pl.BlockSpec((D_MODEL, C_DIM), functools.partial(lambda j, t, c: (0, c * C_HEADS + j), c=c))
               for c in range(4)]
    return pl.pallas_call(
        _hgrn_kernel,
        grid=(C_HEADS, N_HG),
        in_specs=[
            pl.BlockSpec((TM, D_MODEL), c2),
            pl.BlockSpec((TM, D_MODEL), lambda j, t: (jnp.minimum(t + 1, N_HG_PROMPT - 1), 0)),
            pl.BlockSpec((N_SAMPLE, D_MODEL), lambda j, t: (N_PROMPT // N_SAMPLE, 0)),
            *w_specs,
            pl.BlockSpec((2, C_DIM), head_col),
            pl.BlockSpec((1, C_DIM), head_col),
            pl.BlockSpec((HG_CHUNK, HG_CHUNK), c2),
            pl.BlockSpec((HG_CHUNK, HG_CHUNK), c2),
            pl.BlockSpec((DEC_BATCH, None, C_DIM, C_DIM), lambda j, t: (0, j, 0, 0)),
        ],
        out_specs=[
            pl.BlockSpec((TM, C_DIM), lambda j, t: (t, j)),
            pl.BlockSpec((None, C_DIM, C_DIM), lambda j, t: (j, 0, 0)),
            pl.BlockSpec((DEC_BATCH, None, C_DIM, C_DIM), lambda j, t: (0, j, 0, 0)),
        ],
        out_shape=[
            jax.ShapeDtypeStruct((N_TOK, C_HEADS * C_DIM), BF16),
            jax.ShapeDtypeStruct((C_HEADS, C_DIM, C_DIM), F32),
            jax.ShapeDtypeStruct((DEC_BATCH, C_HEADS, C_DIM, C_DIM), F32),
        ],
        scratch_shapes=[
            pltpu.VMEM((D_MODEL, 4 * C_DIM), BF16),
            pltpu.VMEM((TM, 4 * C_DIM), F32),
            pltpu.VMEM((TM, C_DIM), F32),
            pltpu.VMEM((TM, C_DIM), F32),
            pltpu.VMEM((C_DIM, C_DIM), F32),
        ],
        compiler_params=_params(("arbitrary", "arbitrary")),
        name="hgrn",
    )(xn, xn, xn, w_in, w_in, w_in, w_in, lb, g_onorm, tri_p, tri_s, state0)


def _rope_tables():
    half = ROT_DIM // 2
    inv = ROPE_THETA ** (-(jnp.arange(half, dtype=F32) * (2.0 / ROT_DIM)))
    pos = jnp.concatenate([jnp.arange(N_PROMPT, dtype=jnp.int32),
                           jnp.tile(PAST_LEN + jnp.arange(DEC_SEQ, dtype=jnp.int32), DEC_BATCH)])
    ang = pos.astype(F32)[:, None] * inv[None, :]
    cos, sin = jnp.cos(ang), jnp.sin(ang)
    ones = jnp.ones((N_TOK, HEAD_DIM - ROT_DIM), F32)
    ca = jnp.concatenate([cos, cos, ones], axis=1)
    cb = jnp.concatenate([-sin, sin, 0.0 * ones], axis=1)
    return jnp.tile(ca, (1, LANES // HEAD_DIM)), jnp.tile(cb, (1, LANES // HEAD_DIM))


def kernel(x_prompt, x_sample, cache_swa_k, cache_swa_v, state_hgrn, norm_mix, norm_ffn, norm_final,
           w_in_even, w_out_even, attn_sinks, gmlp_vnorm, gmlp_ws, gmlp_bias, w_in_odd, w_out_odd,
           hgrn_lb, hgrn_onorm, w_ffn_up, w_ffn_down):
    xp = x_prompt.reshape(N_PROMPT, D_MODEL)
    xs = x_sample.reshape(N_SAMPLE, D_MODEL)
    ca, cb = _rope_tables()

    q, k, v, u, gv = _even_in(xp, xs, norm_mix[0:1], w_in_even[0], ca, cb, gmlp_vnorm[0:1])
    ws = gmlp_ws[0]
    cidx = jnp.arange(MIX_TM) // CHUNK
    w_p = jnp.where(cidx[:, None] >= cidx[None, :], ws, 0.0).astype(BF16)
    eye = jnp.eye(ITEMS_PER_TILE, dtype=F32)
    w_s = jnp.einsum('ab,gpq->gapbq', eye, ws[:, :DEC_SEQ, :DEC_SEQ]).reshape(B_GROUPS, MIX_TM, MIX_TM).astype(BF16)
    b_p = gmlp_bias[0].T
    b_s = jnp.tile(gmlp_bias[0][:, :DEC_SEQ], (1, ITEMS_PER_TILE)).T
    ab, k_s, v_s = _mixer(attn_sinks[0], q, k, v,
                          cache_swa_k[0].reshape(DEC_BATCH, WINDOW, KV_WIDTH),
                          cache_swa_v[0].reshape(DEC_BATCH, WINDOW, KV_WIDTH),
                          u, gv, w_p, w_s, b_p, b_s)
    h1, xn1 = _proj_ffn((xp, xs), ab, w_out_even[0], norm_ffn[0:1], w_ffn_up, w_ffn_down, 0,
                        norm_mix[1:2], final=False)

    tri_p = jnp.tril(jnp.ones((HG_CHUNK, HG_CHUNK), F32))
    seg = jnp.arange(HG_CHUNK) // DEC_SEQ
    tri_s = jnp.where(seg[:, None] == seg[None, :], tri_p, 0.0)
    og, s_p, s_s = _hgrn(xn1, w_in_odd[0], hgrn_lb, hgrn_onorm[0:1], tri_p.astype(BF16), tri_s.astype(BF16),
                         state_hgrn[0])
    y_p, y_s = _proj_ffn(h1, og, w_out_odd[0], norm_ffn[1:2], w_ffn_up, w_ffn_down, 1,
                         norm_final[None, :], final=True)

    kv_shape = (1, 1, WINDOW, A_KV_HEADS, HEAD_DIM)
    kv_s_shape = (1, DEC_BATCH, WINDOW, A_KV_HEADS, HEAD_DIM)
    return (y_p.reshape(1, N_PROMPT, D_MODEL),
            y_s.reshape(DEC_BATCH, DEC_SEQ, D_MODEL),
            k[N_PROMPT - WINDOW:N_PROMPT].reshape(kv_shape),
            v[N_PROMPT - WINDOW:N_PROMPT].reshape(kv_shape),
            k_s.reshape(kv_s_shape),
            v_s.reshape(kv_s_shape),
            gv[N_PROMPT:].reshape(1, DEC_BATCH, DEC_SEQ, B_GROUPS, GMLP_DIM),
            s_p[None, None],
            s_s[None])
```

```python
import functools
import math

import jax
import jax.numpy as jnp
from jax import lax
from jax.experimental import pallas as pl
from jax.experimental.pallas import tpu as pltpu

F32 = jnp.float32
BF16 = jnp.bfloat16

D_MODEL = 2048
N_PROMPT = 8192
DEC_BATCH = 16
DEC_SEQ = 32
N_SAMPLE = DEC_BATCH * DEC_SEQ
N_TOK = N_PROMPT + N_SAMPLE
PAST_LEN = 2048
EPS = 1e-6

HEAD_DIM = 64
A_HEADS = 16
A_KV_HEADS = 4
A_WIDTH = A_HEADS * HEAD_DIM
KV_WIDTH = A_KV_HEADS * HEAD_DIM
ROT_DIM = 16
ROPE_THETA = 500000.0
WINDOW = 128
CHUNK = 64
B_GROUPS = 8
GMLP_DIM = 128
B_WIDTH = B_GROUPS * GMLP_DIM
EVEN_IN = A_WIDTH + 2 * KV_WIDTH + 2 * B_WIDTH
C_HEADS = 16
C_DIM = 128
D_FF = 4 * D_MODEL

LANES = 128
TM = 2048
TMB = 1088
ROW_CHUNK = 272
N_TILES = N_TOK // TMB
TAIL = N_PROMPT - (N_TILES - 1) * TMB
MIX_TM = 128
HG_CHUNK = 128
IN_BLOCK = 512
PJ_BLOCK = 256
FF_BLOCK = 512
MXU_N = 256
VMEM_LIMIT = 58 * 1024 * 1024

_NT = (((1,), (1,)), ((), ()))


def _params(sem):
    return pltpu.CompilerParams(dimension_semantics=sem, vmem_limit_bytes=VMEM_LIMIT)


def _rms(x, g):
    return x * lax.rsqrt(jnp.mean(x * x, axis=-1, keepdims=True) + EPS) * g


def _sigmoid(x):
    return 0.5 * jnp.tanh(0.5 * x) + 0.5


def _gelu(x):
    c = math.sqrt(2.0 / math.pi)
    return x * (0.5 * (1.0 + jnp.tanh(c * (x + 0.044715 * (x * x * x)))))


def _dot(a, b):
    return jnp.dot(a, b, preferred_element_type=F32)


IN_BLOCKS = EVEN_IN // IN_BLOCK
IN_Q = A_WIDTH // IN_BLOCK
IN_U = IN_Q + 1
IN_V = IN_U + B_WIDTH // IN_BLOCK


def _row_spans(start, stop, step=ROW_CHUNK):
    return [(r, min(r + step, stop)) for r in range(start, stop, step)]


def _even_in_kernel(xp_ref, xs_ref, g_ref, w_ref, ca_ref, cb_ref, gv_g_ref,
                    q_ref, k_ref, v_ref, u_ref, gv_ref, xn_ref):
    i = pl.program_id(0)
    j = pl.program_id(1)

    def norm_rows(src_ref, src0, dst0, n):
        for a, b in _row_spans(0, n):
            xn_ref[dst0 + a:dst0 + b, :] = _rms(src_ref[src0 + a:src0 + b, :], g_ref[...]).astype(BF16)

    @pl.when((j == 0) & (i < N_TILES - 1))
    def _():
        norm_rows(xp_ref, 0, 0, TMB)

    @pl.when((j == 0) & (i == N_TILES - 1))
    def _():
        norm_rows(xp_ref, 0, 0, TAIL)
        norm_rows(xs_ref, 0, TAIL, N_SAMPLE)

    def rope(x):
        lane = lax.broadcasted_iota(jnp.int32, (TMB, LANES), 1)
        first = (lane & (HEAD_DIM - 1)) < (ROT_DIM // 2)
        partner = jnp.where(first, pltpu.roll(x, LANES - ROT_DIM // 2, 1), pltpu.roll(x, ROT_DIM // 2, 1))
        return x * ca_ref[...] + partner * cb_ref[...]

    tiles = [slice(t * LANES, (t + 1) * LANES) for t in range(IN_BLOCK // LANES)]

    def epilogue(c, z):
        if c < IN_Q:
            for sl in tiles:
                q_ref[:, sl] = (rope(z[:, sl]) * (HEAD_DIM ** -0.5)).astype(BF16)
        elif c == IN_Q:
            for sl in tiles[:KV_WIDTH // LANES]:
                k_ref[:, sl] = rope(z[:, sl])
            v_ref[...] = z[:, KV_WIDTH:]
        elif c < IN_V:
            u_ref[...] = _gelu(z).astype(BF16)
        else:
            for sl in tiles:
                x = _gelu(z[:, sl])
                xc = x - jnp.mean(x, axis=-1, keepdims=True)
                y = xc * lax.rsqrt(jnp.mean(xc * xc, axis=-1, keepdims=True) + EPS)
                gv_ref[:, sl] = y * gv_g_ref[:, sl]

    for c in range(IN_BLOCKS):
        @pl.when(j == c)
        def _():
            epilogue(c, _dot(xn_ref[...], w_ref[...].astype(BF16)))


def _even_in(x_prompt, x_sample, g, w, ca, cb, gv_g):
    row = lambda i, j: (i, 0)
    const = lambda i, j: (0, 0)
    ahead = lambda i, j: (jnp.minimum(i + (j > IN_Q).astype(jnp.int32), N_TILES - 1), 0)
    done = lambda j, first, count: jnp.clip(j - first, 0, count - 1)
    return pl.pallas_call(
        _even_in_kernel,
        grid=(N_TILES, IN_BLOCKS),
        in_specs=[
            pl.BlockSpec((TMB, D_MODEL), ahead),
            pl.BlockSpec((N_SAMPLE, D_MODEL), const),
            pl.BlockSpec((1, D_MODEL), const),
            pl.BlockSpec((D_MODEL, IN_BLOCK), lambda i, j: (0, jnp.minimum(j, IN_BLOCKS - 1))),
            pl.BlockSpec((TMB, LANES), ahead),
            pl.BlockSpec((TMB, LANES), ahead),
            pl.BlockSpec((1, IN_BLOCK), lambda i, j: (0, done(j, IN_V, IN_BLOCKS - IN_V))),
        ],
        out_specs=[
            pl.BlockSpec((TMB, IN_BLOCK), lambda i, j: (i, done(j, 0, IN_Q))),
            pl.BlockSpec((TMB, KV_WIDTH), row),
            pl.BlockSpec((TMB, KV_WIDTH), row),
            pl.BlockSpec((TMB, IN_BLOCK), lambda i, j: (i, done(j, IN_U, IN_V - IN_U))),
            pl.BlockSpec((TMB, IN_BLOCK), lambda i, j: (i, done(j, IN_V, IN_BLOCKS - IN_V))),
        ],
        out_shape=[
            jax.ShapeDtypeStruct((N_TOK, A_WIDTH), BF16),
            jax.ShapeDtypeStruct((N_TOK, KV_WIDTH), F32),
            jax.ShapeDtypeStruct((N_TOK, KV_WIDTH), F32),
            jax.ShapeDtypeStruct((N_TOK, B_WIDTH), BF16),
            jax.ShapeDtypeStruct((N_TOK, B_WIDTH), F32),
        ],
        scratch_shapes=[pltpu.VMEM((TMB, D_MODEL), BF16)],
        compiler_params=_params(("arbitrary", "arbitrary")),
        name="even_in",
    )(x_prompt, x_sample, g, w, ca, cb, gv_g)


N_MIX_PROMPT = N_PROMPT // MIX_TM
ITEMS_PER_TILE = MIX_TM // DEC_SEQ
N_MIX = N_TOK // MIX_TM


def _spread_heads(x, kv_head):
    t = x[:, (kv_head // 2) * LANES:(kv_head // 2 + 1) * LANES]
    lane = lax.broadcasted_iota(jnp.int32, t.shape, 1)
    if kv_head % 2 == 0:
        lo = jnp.where(lane < HEAD_DIM, t, 0.0)
        hi = pltpu.roll(lo, HEAD_DIM, 1)
    else:
        hi = jnp.where(lane >= HEAD_DIM, t, 0.0)
        lo = pltpu.roll(hi, HEAD_DIM, 1)
    return lo, hi


def _kv_operands(kk, vv):
    s_len = kk[0].shape[0]
    kx = jnp.concatenate(kk, axis=0).astype(BF16)
    vx = jnp.concatenate(vv, axis=0)
    row = lax.broadcasted_iota(jnp.int32, vx.shape, 0)
    lane = lax.broadcasted_iota(jnp.int32, vx.shape, 1)
    own = ((row < s_len) & (lane < HEAD_DIM)) | ((row >= s_len) & (lane >= HEAD_DIM))
    ones = jnp.where(own, 1.0, 0.0)
    return kx, jnp.concatenate([vx, ones], axis=1).astype(BF16)


def _attend(q2, kx, vx1, sink_a, sink_b, allowed):
    s_len = kx.shape[0] // 2
    s = lax.dot_general(q2, kx, _NT, preferred_element_type=F32)
    ps, sink_terms = [], []
    for half, sink in ((0, sink_a), (1, sink_b)):
        sh = s[:, half * s_len:(half + 1) * s_len]
        if allowed is not None:
            sh = jnp.where(allowed, sh, -jnp.inf)
        m = jnp.maximum(jnp.max(sh, axis=-1, keepdims=True), sink)
        ps.append(jnp.exp(sh - m))
        sink_terms.append(jnp.exp(sink - m))
    o = _dot(jnp.concatenate(ps, axis=1).astype(BF16), vx1)
    lane = lax.broadcasted_iota(jnp.int32, (o.shape[0], LANES), 1)
    return o[:, :LANES] / (o[:, LANES:] + jnp.where(lane < HEAD_DIM, sink_terms[0], sink_terms[1]))


def _mixer_kernel(sink_ref, q_ref, kp_ref, kc_ref, vp_ref, vc_ref, ck_ref, cv_ref,
                  u_ref, gv_ref, wp_ref, ws_ref, bp_ref, bs_ref,
                  ab_ref, ko_ref, vo_ref):
    i = pl.program_id(0)

    def gmlp(w_ref, b_ref):
        for g in range(B_GROUPS):
            sl = slice(g * GMLP_DIM, (g + 1) * GMLP_DIM)
            mixed = _dot(w_ref[g], gv_ref[:, sl].astype(BF16)) + b_ref[:, g:g + 1]
            ab_ref[:, A_WIDTH + g * GMLP_DIM:A_WIDTH + (g + 1) * GMLP_DIM] = (
                u_ref[:, sl].astype(F32) * mixed).astype(BF16)

    @pl.when(i < N_MIX_PROMPT)
    def _prompt():
        kk = jnp.concatenate([kp_ref[...], kc_ref[...]], axis=0)
        vv = jnp.concatenate([vp_ref[...], vc_ref[...]], axis=0)
        row = lax.broadcasted_iota(jnp.int32, (MIX_TM, 2 * MIX_TM), 0)
        col = lax.broadcasted_iota(jnp.int32, (MIX_TM, 2 * MIX_TM), 1)
        allowed = ((row < CHUNK) & (col < 3 * CHUNK)) | ((row >= CHUNK) & (col >= CHUNK))
        allowed = allowed & ((col >= MIX_TM) | (i > 0))
        for j in range(A_KV_HEADS):
            kx, vx1 = _kv_operands(_spread_heads(kk, j), _spread_heads(vv, j))
            for p in (2 * j, 2 * j + 1):
                sl = slice(p * LANES, (p + 1) * LANES)
                o = _attend(q_ref[:, sl], kx, vx1, sink_ref[2 * p], sink_ref[2 * p + 1], allowed)
                ab_ref[:, sl] = o.astype(BF16)
        gmlp(wp_ref, bp_ref)

    @pl.when(i >= N_MIX_PROMPT)
    def _sample():
        span = WINDOW + DEC_SEQ
        k_parts, v_parts = [], []
        for a in range(ITEMS_PER_TILE):
            rows = slice(a * DEC_SEQ, (a + 1) * DEC_SEQ)
            k_parts += [ck_ref[a], kc_ref[rows, :]]
            v_parts += [cv_ref[a], vc_ref[rows, :]]
            ko_ref[a] = jnp.concatenate([ck_ref[a, DEC_SEQ:, :], kc_ref[rows, :]], axis=0)
            vo_ref[a] = jnp.concatenate([cv_ref[a, DEC_SEQ:, :], vc_ref[rows, :]], axis=0)
        kk = jnp.concatenate(k_parts, axis=0)
        vv = jnp.concatenate(v_parts, axis=0)
        row = lax.broadcasted_iota(jnp.int32, (MIX_TM, ITEMS_PER_TILE * span), 0)
        col = lax.broadcasted_iota(jnp.int32, (MIX_TM, ITEMS_PER_TILE * span), 1)
        allowed = None
        for a in range(ITEMS_PER_TILE):
            own = ((row >= a * DEC_SEQ) & (row < (a + 1) * DEC_SEQ)
                   & (col >= a * span) & (col < (a + 1) * span))
            allowed = own if allowed is None else allowed | own
        for j in range(A_KV_HEADS):
            kx, vx1 = _kv_operands(_spread_heads(kk, j), _spread_heads(vv, j))
            for p in (2 * j, 2 * j + 1):
                sl = slice(p * LANES, (p + 1) * LANES)
                o = _attend(q_ref[:, sl], kx, vx1, sink_ref[2 * p], sink_ref[2 * p + 1], allowed)
                ab_ref[:, sl] = o.astype(BF16)
        gmlp(ws_ref, bs_ref)


def _mixer(sinks, q, k, v, cache_k, cache_v, u, gv, w_p, w_s, b_p, b_s):
    row = lambda i: (i, 0)
    prev = lambda i: (jnp.maximum(i - 1, 0), 0)
    item = lambda i: (jnp.maximum(i - N_MIX_PROMPT, 0), 0, 0)
    c2 = lambda i: (0, 0)
    c3 = lambda i: (0, 0, 0)
    cache_block = (ITEMS_PER_TILE, WINDOW, KV_WIDTH)
    return pl.pallas_call(
        _mixer_kernel,
        grid=(N_MIX,),
        in_specs=[
            pl.BlockSpec(memory_space=pltpu.SMEM),
            pl.BlockSpec((MIX_TM, A_WIDTH), row),
            pl.BlockSpec((MIX_TM, KV_WIDTH), prev),
            pl.BlockSpec((MIX_TM, KV_WIDTH), row),
            pl.BlockSpec((MIX_TM, KV_WIDTH), prev),
            pl.BlockSpec((MIX_TM, KV_WIDTH), row),
            pl.BlockSpec(cache_block, item),
            pl.BlockSpec(cache_block, item),
            pl.BlockSpec((MIX_TM, B_WIDTH), row),
            pl.BlockSpec((MIX_TM, B_WIDTH), row),
            pl.BlockSpec((B_GROUPS, MIX_TM, MIX_TM), c3),
            pl.BlockSpec((B_GROUPS, MIX_TM, MIX_TM), c3),
            pl.BlockSpec((MIX_TM, B_GROUPS), c2),
            pl.BlockSpec((MIX_TM, B_GROUPS), c2),
        ],
        out_specs=[
            pl.BlockSpec((MIX_TM, A_WIDTH + B_WIDTH), row),
            pl.BlockSpec(cache_block, item),
            pl.BlockSpec(cache_block, item),
        ],
        out_shape=[
            jax.ShapeDtypeStruct((N_TOK, A_WIDTH + B_WIDTH), BF16),
            jax.ShapeDtypeStruct((DEC_BATCH, WINDOW, KV_WIDTH), F32),
            jax.ShapeDtypeStruct((DEC_BATCH, WINDOW, KV_WIDTH), F32),
        ],
        compiler_params=_params(("arbitrary",)),
        name="mixer",
    )(sinks, q, k, k, v, v, cache_k, cache_v, u, gv, w_p, w_s, b_p, b_s)


N_PJ = D_MODEL // PJ_BLOCK
N_FF = D_FF // FF_BLOCK


def _proj_ffn_kernel(*refs, split_in, final):
    refs = list(refs)
    h_refs = [refs.pop(0) for _ in range(2 if split_in else 1)]
    y_ref, wo_ref, gf_ref, wu_ref, wd_ref, gn_ref, o_ref = refs[:7]
    if final:
        ys_ref, xn_ref = refs[7:]
    else:
        (xn_ref,) = refs[7:]
    i = pl.program_id(0)
    s = pl.program_id(1)
    last_tile = i == N_TILES - 1
    cols = [slice(c * MXU_N, (c + 1) * MXU_N) for c in range(D_MODEL // MXU_N)]

    def accumulate(lhs, w, first=False):
        for sl in cols:
            if first:
                o_ref[:, sl] = _dot(lhs, w[:, sl])
            else:
                o_ref[:, sl] += _dot(lhs, w[:, sl])

    @pl.when(s == 0)
    def _():
        accumulate(y_ref[...], wo_ref[...].astype(BF16), first=True)

    @pl.when((s > 0) & (s < N_PJ))
    def _():
        accumulate(y_ref[...], wo_ref[...].astype(BF16))

    for c in range(N_PJ):
        sl = slice(c * PJ_BLOCK, (c + 1) * PJ_BLOCK)
        if split_in:
            hp_ref, hs_ref = h_refs

            @pl.when((s == c) & jnp.logical_not(last_tile))
            def _():
                o_ref[:, sl] += hp_ref[...]

            @pl.when((s == c) & last_tile)
            def _():
                o_ref[0:TAIL, sl] += hp_ref[0:TAIL, :]
                o_ref[TAIL:, sl] += hs_ref[...]
        else:
            @pl.when(s == c)
            def _():
                o_ref[:, sl] += h_refs[0][...]

    @pl.when(s == N_PJ)
    def _():
        for a, b in _row_spans(0, TMB):
            xn_ref[a:b, :] = _rms(o_ref[a:b, :], gf_ref[...]).astype(BF16)

    @pl.when(s >= N_PJ)
    def _():
        t = jnp.maximum(_dot(xn_ref[...], wu_ref[...].astype(BF16)), 0.0)
        accumulate((t * t).astype(BF16), wd_ref[...].astype(BF16))

    last_step = s == N_PJ + N_FF - 1
    if final:
        @pl.when(last_step & jnp.logical_not(last_tile))
        def _():
            for a, b in _row_spans(0, TMB):
                o_ref[a:b, :] = _rms(o_ref[a:b, :], gn_ref[...])

        @pl.when(last_step & last_tile)
        def _():
            for a, b in _row_spans(0, TAIL):
                o_ref[a:b, :] = _rms(o_ref[a:b, :], gn_ref[...])
            for a, b in _row_spans(0, N_SAMPLE):
                ys_ref[a:b, :] = _rms(o_ref[TAIL + a:TAIL + b, :], gn_ref[...])
    else:
        @pl.when(last_step)
        def _():
            for a, b in _row_spans(0, TMB):
                xn_ref[a:b, :] = _rms(o_ref[a:b, :], gn_ref[...]).astype(BF16)


def _proj_ffn(h, y, w_out, g_ffn, w_up, w_down, layer, g_next, final):
    split_in = isinstance(h, tuple)
    row = lambda i, s: (i, 0)
    const = lambda i, s: (0, 0)
    pj = lambda s: jnp.minimum(s, N_PJ - 1)
    ff = lambda s: jnp.maximum(s - N_PJ, 0)
    h_specs = [pl.BlockSpec((TMB, PJ_BLOCK), lambda i, s: (i, pj(s)))]
    if split_in:
        h_specs.append(pl.BlockSpec((N_SAMPLE, PJ_BLOCK), lambda i, s: (0, pj(s))))
    if final:
        out_specs = [pl.BlockSpec((TMB, D_MODEL), row), pl.BlockSpec((N_SAMPLE, D_MODEL), const)]
        out_shape = [jax.ShapeDtypeStruct((N_PROMPT, D_MODEL), F32),
                     jax.ShapeDtypeStruct((N_SAMPLE, D_MODEL), F32)]
        scratch = [pltpu.VMEM((TMB, D_MODEL), BF16)]
    else:
        out_specs = [pl.BlockSpec((TMB, D_MODEL), row), pl.BlockSpec((TMB, D_MODEL), row)]
        out_shape = [jax.ShapeDtypeStruct((N_TOK, D_MODEL), F32),
                     jax.ShapeDtypeStruct((N_TOK, D_MODEL), BF16)]
        scratch = []
    return pl.pallas_call(
        functools.partial(_proj_ffn_kernel, split_in=split_in, final=final),
        grid=(N_TILES, N_PJ + N_FF),
        in_specs=[
            *h_specs,
            pl.BlockSpec((TMB, PJ_BLOCK), lambda i, s: (i, pj(s))),
            pl.BlockSpec((PJ_BLOCK, D_MODEL), lambda i, s: (pj(s), 0)),
            pl.BlockSpec((1, D_MODEL), const),
            pl.BlockSpec((None, D_MODEL, FF_BLOCK), lambda i, s: (layer, 0, ff(s))),
            pl.BlockSpec((None, FF_BLOCK, D_MODEL), lambda i, s: (layer, ff(s), 0)),
            pl.BlockSpec((1, D_MODEL), const),
        ],
        out_specs=out_specs,
        out_shape=out_shape,
        scratch_shapes=scratch,
        compiler_params=_params(("arbitrary", "arbitrary")),
        name="proj_ffn_final" if final else "proj_ffn",
    )(*(h if split_in else (h,)), y, w_out, g_ffn, w_up, w_down, g_next)


N_HG_PROMPT = N_PROMPT // TM
N_HG = N_HG_PROMPT + 1
CHUNKS_PER_TILE = TM // HG_CHUNK
SAMPLE_CHUNKS = N_SAMPLE // HG_CHUNK


def _hgrn_scores(q, k, f, b_ref, r0, seg):
    n = HG_CHUNK
    sub = 8
    apart = [(lax.broadcasted_iota(jnp.int32, (sub, n), 0) + r * sub)
             ^ lax.broadcasted_iota(jnp.int32, (sub, n), 1) for r in range(n // sub)]
    scores = [None] * (n // sub)

    def merge(r, h, block):
        if scores[r] is None:
            scores[r] = block if 2 * h == n else jnp.where(apart[r] < 2 * h, block, 0.0)
        else:
            scores[r] = jnp.where(apart[r] < 2 * h, block, scores[r])

    def coarse(h):
        qs, ks = [], []
        zero = jnp.zeros((h, C_DIM), F32)
        for g in range(n // (2 * h)):
            base = r0 + g * 2 * h
            beta = b_ref[base + h - 1:base + h, :]
            lo = slice(g * 2 * h, g * 2 * h + h)
            up = slice(g * 2 * h + h, (g + 1) * 2 * h)
            ks += [k[lo] * jnp.exp2(beta - b_ref[base:base + h, :]), zero]
            qs.append(q[up] * jnp.exp2(b_ref[base + h:base + 2 * h, :] - beta))
        return jnp.concatenate(qs, axis=0).astype(BF16), jnp.concatenate(ks, axis=0).astype(BF16)

    def fine(h):
        if h == 4:
            qs, ks = [], []
            upper = lax.broadcasted_iota(jnp.int32, (8, C_DIM), 0) >= 4
            for g in range(n // 8):
                sl = slice(8 * g, 8 * g + 8)
                base = r0 + 8 * g
                e = jnp.exp2(-jnp.abs(b_ref[base:base + 8, :] - b_ref[base + 3:base + 4, :]))
                qs.append(jnp.where(upper, q[sl] * e, 0.0))
                ks.append(jnp.where(upper, 0.0, k[sl] * e))
            qt = jnp.concatenate(qs, axis=0)
            kt = jnp.concatenate(ks, axis=0)
        elif h == 2:
            r4 = lax.broadcasted_iota(jnp.int32, (n, C_DIM), 0) & 3
            f_next = pltpu.roll(f, n - 1, 0)
            f_prev = pltpu.roll(f, 1, 0)
            kt = jnp.where(r4 == 0, k * f_next, jnp.where(r4 == 1, k, 0.0))
            qt = jnp.where(r4 == 2, q * f, jnp.where(r4 == 3, q * f * f_prev, 0.0))
        else:
            odd = (lax.broadcasted_iota(jnp.int32, (n, C_DIM), 0) & 1) == 1
            kt = jnp.where(odd, 0.0, k)
            qt = jnp.where(odd, q * f, 0.0)
        return qt.astype(BF16), kt.astype(BF16)

    h = seg // 2
    while h >= 1:
        qt, kt = coarse(h) if h >= 8 else fine(h)
        s_h = lax.dot_general(qt, kt, _NT, preferred_element_type=F32)
        if h >= 8:
            per_group = h // sub
            for i in range(n // 2 // sub):
                g, j = divmod(i, per_group)
                merge((g * 2 * h + h) // sub + j, h, s_h[i * sub:(i + 1) * sub])
        else:
            for r in range(n // sub):
                merge(r, h, s_h[r * sub:(r + 1) * sub])
        h //= 2
    diag = jnp.sum(q * k, axis=1, keepdims=True)
    blocks = []
    for r in range(n // sub):
        lower = jnp.zeros((sub, n), F32) if scores[r] is None else scores[r]
        blocks.append(jnp.where(apart[r] == 0, diag[r * sub:(r + 1) * sub], lower))
    return jnp.concatenate(blocks, axis=0)


def _hgrn_kernel(xn_ref, xn_s_ref, wq_ref, wf_ref, wi_ref, wg_ref, lb_ref, go_ref, trip_ref,
                 tris_ref, s0_ref, og_ref, sp_ref, ss_ref, w_ref, z_ref, f_ref, b_ref, st_ref):
    t = pl.program_id(1)

    lbx = lb_ref[...]
    e = jnp.exp(lbx - jnp.max(lbx, axis=0, keepdims=True))
    p = e / jnp.sum(e, axis=0, keepdims=True)
    lb = (p[0:1, :] + p[1:2, :]) - p[0:1, :]

    @pl.when(t == 0)
    def _():
        st_ref[...] = jnp.zeros_like(st_ref)
        for c, src in enumerate((wq_ref, wf_ref, wi_ref, wg_ref)):
            w_ref[:, c * C_DIM:(c + 1) * C_DIM] = src[...].astype(BF16)

    def cumulative_log_decay(tri_ref, n_chunks):
        lf = jnp.log2(f_ref[0:n_chunks * HG_CHUNK, :])
        lf = jnp.concatenate([lf[c * HG_CHUNK:(c + 1) * HG_CHUNK] for c in range(n_chunks)], axis=1)
        hi = lf.astype(BF16)
        lo = (lf - hi.astype(F32)).astype(BF16)
        bb = _dot(tri_ref[...], jnp.concatenate([hi, lo], axis=1))
        for c in range(n_chunks):
            lanes = slice(c * C_DIM, (c + 1) * C_DIM)
            b_ref[c * HG_CHUNK:(c + 1) * HG_CHUNK, :] = bb[:, lanes] + bb[:, n_chunks * C_DIM:][:, lanes]

    def chunk(c, seg):
        r0 = c * HG_CHUNK
        rows = slice(r0, r0 + HG_CHUNK)
        zq = z_ref[rows, 0:C_DIM]
        f = f_ref[rows, :]
        v = z_ref[rows, 2 * C_DIM:3 * C_DIM]
        gate = _sigmoid(z_ref[rows, 3 * C_DIM:4 * C_DIM])
        q = zq * _sigmoid(zq)
        k = 1.0 - f
        scores = _hgrn_scores(q, k, f, b_ref, r0, seg)
        o = _dot(scores.astype(BF16), v.astype(BF16))
        b = b_ref[rows, :]
        qe = (q * jnp.exp2(b)).astype(BF16)
        if seg == HG_CHUNK:
            s_t = st_ref[...]
            o = o + lax.dot_general(qe, s_t.astype(BF16), _NT, preferred_element_type=F32)
            b_last = b_ref[r0 + HG_CHUNK - 1:r0 + HG_CHUNK, :]
            ke = (k * jnp.exp2(b_last - b)).astype(BF16)
            st_ref[...] = s_t * jnp.exp2(b_last) + _dot(v.T.astype(BF16), ke)
        else:
            outs = []
            v_t = v.T.astype(BF16)
            row = lax.broadcasted_iota(jnp.int32, (HG_CHUNK, C_DIM), 0)
            for a in range(HG_CHUNK // seg):
                item = c * (HG_CHUNK // seg) + a
                r = slice(a * seg, (a + 1) * seg)
                s_t = s0_ref[item].T
                outs.append(lax.dot_general(qe[r], s_t.astype(BF16), _NT, preferred_element_type=F32))
                b_last = b_ref[r0 + (a + 1) * seg - 1:r0 + (a + 1) * seg, :]
                own = (row >= a * seg) & (row < (a + 1) * seg)
                ke = jnp.where(own, k * jnp.exp2(jnp.minimum(b_last - b, 0.0)), 0.0).astype(BF16)
                s_new = s_t * jnp.exp2(b_last) + _dot(v_t, ke)
                ss_ref[item] = s_new.T
            o = o + jnp.concatenate(outs, axis=0)
        y = _rms(o, go_ref[...]) * gate
        og_ref[rows, :] = y.astype(BF16)

    def tile(src_ref, tri_ref, seg, n_chunks):
        rows = slice(0, n_chunks * HG_CHUNK)
        z_ref[rows, :] = _dot(src_ref[...], w_ref[...])
        f_ref[rows, :] = lb + (1.0 - lb) * _sigmoid(z_ref[rows, C_DIM:2 * C_DIM])
        cumulative_log_decay(tri_ref, n_chunks)
        for c in range(n_chunks):
            chunk(c, seg)

    @pl.when(t < N_HG_PROMPT)
    def _():
        tile(xn_ref, trip_ref, HG_CHUNK, CHUNKS_PER_TILE)

    @pl.when(t == N_HG_PROMPT - 1)
    def _():
        sp_ref[...] = st_ref[...].T

    @pl.when(t == N_HG_PROMPT)
    def _():
        tile(xn_s_ref, tris_ref, DEC_SEQ, SAMPLE_CHUNKS)


def _hgrn(xn, w_in, lb, g_onorm, tri_p, tri_s, state0):
    c2 = lambda j, t: (0, 0)
    head_col = lambda j, t: (0, j)
    w_specs = [pl.BlockSpec((D_MODEL, C_DIM), functools.partial(lambda j, t, c: (0, c * C_HEADS + j), c=c))
               for c in range(4)]
    return pl.pallas_call(
        _hgrn_kernel,
        grid=(C_HEADS, N_HG),
        in_specs=[
            pl.BlockSpec((TM, D_MODEL), lambda j, t: (jnp.minimum(t, N_HG_PROMPT - 1), 0)),
            pl.BlockSpec((N_SAMPLE, D_MODEL), lambda j, t: (N_PROMPT // N_SAMPLE, 0)),
            *w_specs,
            pl.BlockSpec((2, C_DIM), head_col),
            pl.BlockSpec((1, C_DIM), head_col),
            pl.BlockSpec((HG_CHUNK, HG_CHUNK), c2),
            pl.BlockSpec((HG_CHUNK, HG_CHUNK), c2),
            pl.BlockSpec((DEC_BATCH, None, C_DIM, C_DIM), lambda j, t: (0, j, 0, 0)),
        ],
        out_specs=[
            pl.BlockSpec((TM, C_DIM), lambda j, t: (t, j)),
            pl.BlockSpec((None, C_DIM, C_DIM), lambda j, t: (j, 0, 0)),
            pl.BlockSpec((DEC_BATCH, None, C_DIM, C_DIM), lambda j, t: (0, j, 0, 0)),
        ],
        out_shape=[
            jax.ShapeDtypeStruct((N_TOK, C_HEADS * C_DIM), BF16),
            jax.ShapeDtypeStruct((C_HEADS, C_DIM, C_DIM), F32),
            jax.ShapeDtypeStruct((DEC_BATCH, C_HEADS, C_DIM, C_DIM), F32),
        ],
        scratch_shapes=[
            pltpu.VMEM((D_MODEL, 4 * C_DIM), BF16),
            pltpu.VMEM((TM, 4 * C_DIM), F32),
            pltpu.VMEM((TM, C_DIM), F32),
            pltpu.VMEM((TM, C_DIM), F32),
            pltpu.VMEM((C_DIM, C_DIM), F32),
        ],
        compiler_params=_params(("arbitrary", "arbitrary")),
        name="hgrn",
    )(xn, xn, w_in, w_in, w_in, w_in, lb, g_onorm, tri_p, tri_s, state0)


def _rope_tables():
    half = ROT_DIM // 2
    inv = ROPE_THETA ** (-(jnp.arange(half, dtype=F32) * (2.0 / ROT_DIM)))
    pos = jnp.concatenate([jnp.arange(N_PROMPT, dtype=jnp.int32),
                           jnp.tile(PAST_LEN + jnp.arange(DEC_SEQ, dtype=jnp.int32), DEC_BATCH)])
    ang = pos.astype(F32)[:, None] * inv[None, :]
    cos, sin = jnp.cos(ang), jnp.sin(ang)
    ones = jnp.ones((N_TOK, HEAD_DIM - ROT_DIM), F32)
    ca = jnp.concatenate([cos, cos, ones], axis=1)
    cb = jnp.concatenate([-sin, sin, 0.0 * ones], axis=1)
    return jnp.tile(ca, (1, LANES // HEAD_DIM)), jnp.tile(cb, (1, LANES // HEAD_DIM))


def kernel(x_prompt, x_sample, cache_swa_k, cache_swa_v, state_hgrn, norm_mix, norm_ffn, norm_final,
           w_in_even, w_out_even, attn_sinks, gmlp_vnorm, gmlp_ws, gmlp_bias, w_in_odd, w_out_odd,
           hgrn_lb, hgrn_onorm, w_ffn_up, w_ffn_down):
    xp = x_prompt.reshape(N_PROMPT, D_MODEL)
    xs = x_sample.reshape(N_SAMPLE, D_MODEL)
    ca, cb = _rope_tables()

    q, k, v, u, gv = _even_in(xp, xs, norm_mix[0:1], w_in_even[0], ca, cb, gmlp_vnorm[0:1])
    ws = gmlp_ws[0]
    cidx = jnp.arange(MIX_TM) // CHUNK
    w_p = jnp.where(cidx[:, None] >= cidx[None, :], ws, 0.0).astype(BF16)
    eye = jnp.eye(ITEMS_PER_TILE, dtype=F32)
    w_s = jnp.einsum('ab,gpq->gapbq', eye, ws[:, :DEC_SEQ, :DEC_SEQ]).reshape(B_GROUPS, MIX_TM, MIX_TM).astype(BF16)
    b_p = gmlp_bias[0].T
    b_s = jnp.tile(gmlp_bias[0][:, :DEC_SEQ], (1, ITEMS_PER_TILE)).T
    ab, k_s, v_s = _mixer(attn_sinks[0], q, k, v,
                          cache_swa_k[0].reshape(DEC_BATCH, WINDOW, KV_WIDTH),
                          cache_swa_v[0].reshape(DEC_BATCH, WINDOW, KV_WIDTH),
                          u, gv, w_p, w_s, b_p, b_s)
    h1, xn1 = _proj_ffn((xp, xs), ab, w_out_even[0], norm_ffn[0:1], w_ffn_up, w_ffn_down, 0,
                        norm_mix[1:2], final=False)

    tri_p = jnp.tril(jnp.ones((HG_CHUNK, HG_CHUNK), F32))
    seg = jnp.arange(HG_CHUNK) // DEC_SEQ
    tri_s = jnp.where(seg[:, None] == seg[None, :], tri_p, 0.0)
    og, s_p, s_s = _hgrn(xn1, w_in_odd[0], hgrn_lb, hgrn_onorm[0:1], tri_p.astype(BF16), tri_s.astype(BF16),
                         state_hgrn[0])
    y_p, y_s = _proj_ffn(h1, og, w_out_odd[0], norm_ffn[1:2], w_ffn_up, w_ffn_down, 1,
                         norm_final[None, :], final=True)

    kv_shape = (1, 1, WINDOW, A_KV_HEADS, HEAD_DIM)
    kv_s_shape = (1, DEC_BATCH, WINDOW, A_KV_HEADS, HEAD_DIM)
    return (y_p.reshape(1, N_PROMPT, D_MODEL),
            y_s.reshape(DEC_BATCH, DEC_SEQ, D_MODEL),
            k[N_PROMPT - WINDOW:N_PROMPT].reshape(kv_shape),
            v[N_PROMPT - WINDOW:N_PROMPT].reshape(kv_shape),
            k_s.reshape(kv_s_shape),
            v_s.reshape(kv_s_shape),
            gv[N_PROMPT:].reshape(1, DEC_BATCH, DEC_SEQ, B_GROUPS, GMLP_DIM),
            s_p[None, None],
            s_s[None])
```

```python
import functools
import math

import jax
import jax.numpy as jnp
from jax import lax
from jax.experimental import pallas as pl
from jax.experimental.pallas import tpu as pltpu

F32 = jnp.float32
BF16 = jnp.bfloat16

D_MODEL = 2048
N_PROMPT = 8192
DEC_BATCH = 16
DEC_SEQ = 32
N_SAMPLE = DEC_BATCH * DEC_SEQ
N_TOK = N_PROMPT + N_SAMPLE
PAST_LEN = 2048
EPS = 1e-6

HEAD_DIM = 64
A_HEADS = 16
A_KV_HEADS = 4
A_WIDTH = A_HEADS * HEAD_DIM
KV_WIDTH = A_KV_HEADS * HEAD_DIM
ROT_DIM = 16
ROPE_THETA = 500000.0
WINDOW = 128
CHUNK = 64
B_GROUPS = 8
GMLP_DIM = 128
B_WIDTH = B_GROUPS * GMLP_DIM
EVEN_IN = A_WIDTH + 2 * KV_WIDTH + 2 * B_WIDTH
C_HEADS = 16
C_DIM = 128
D_FF = 4 * D_MODEL

LANES = 128
TM = 2048
TMB = 1088
ROW_CHUNK = 272
N_TILES = N_TOK // TMB
TAIL = N_PROMPT - (N_TILES - 1) * TMB
MIX_TM = 128
HG_CHUNK = 128
IN_BLOCK = 512
PJ_BLOCK = 512
FF_BLOCK = 512
MXU_N = 256
VMEM_LIMIT = 61 * 1024 * 1024

_NT = (((1,), (1,)), ((), ()))


def _params(sem):
    return pltpu.CompilerParams(dimension_semantics=sem, vmem_limit_bytes=VMEM_LIMIT)


def _rms(x, g):
    return x * lax.rsqrt(jnp.mean(x * x, axis=-1, keepdims=True) + EPS) * g


def _sigmoid(x):
    return 0.5 * jnp.tanh(0.5 * x) + 0.5


def _gelu(x):
    c = math.sqrt(2.0 / math.pi)
    return x * (0.5 * (1.0 + jnp.tanh(c * (x + 0.044715 * (x * x * x)))))


def _dot(a, b):
    return jnp.dot(a, b, preferred_element_type=F32)


IN_BLOCKS = EVEN_IN // IN_BLOCK
IN_Q = A_WIDTH // IN_BLOCK
IN_U = IN_Q + 1
IN_V = IN_U + B_WIDTH // IN_BLOCK


def _row_spans(start, stop, step=ROW_CHUNK):
    return [(r, min(r + step, stop)) for r in range(start, stop, step)]


def _even_in_kernel(xp_ref, xs_ref, g_ref, w_ref, ca_ref, cb_ref, gv_g_ref,
                    q_ref, k_ref, v_ref, u_ref, gv_ref, xn_ref):
    i = pl.program_id(0)
    j = pl.program_id(1)

    def norm_rows(src_ref, src0, dst0, n):
        for a, b in _row_spans(0, n):
            xn_ref[dst0 + a:dst0 + b, :] = _rms(src_ref[src0 + a:src0 + b, :], g_ref[...]).astype(BF16)

    @pl.when((j == 0) & (i < N_TILES - 1))
    def _():
        norm_rows(xp_ref, 0, 0, TMB)

    @pl.when((j == 0) & (i == N_TILES - 1))
    def _():
        norm_rows(xp_ref, 0, 0, TAIL)
        norm_rows(xs_ref, 0, TAIL, N_SAMPLE)

    def rope(x):
        lane = lax.broadcasted_iota(jnp.int32, (TMB, LANES), 1)
        first = (lane & (HEAD_DIM - 1)) < (ROT_DIM // 2)
        partner = jnp.where(first, pltpu.roll(x, LANES - ROT_DIM // 2, 1), pltpu.roll(x, ROT_DIM // 2, 1))
        return x * ca_ref[...] + partner * cb_ref[...]

    tiles = [slice(t * LANES, (t + 1) * LANES) for t in range(IN_BLOCK // LANES)]

    def epilogue(c, z):
        if c < IN_Q:
            for sl in tiles:
                q_ref[:, sl] = (rope(z[:, sl]) * (HEAD_DIM ** -0.5)).astype(BF16)
        elif c == IN_Q:
            for sl in tiles[:KV_WIDTH // LANES]:
                k_ref[:, sl] = rope(z[:, sl])
            v_ref[...] = z[:, KV_WIDTH:]
        elif c < IN_V:
            u_ref[...] = _gelu(z).astype(BF16)
        else:
            for sl in tiles:
                x = _gelu(z[:, sl])
                xc = x - jnp.mean(x, axis=-1, keepdims=True)
                y = xc * lax.rsqrt(jnp.mean(xc * xc, axis=-1, keepdims=True) + EPS)
                gv_ref[:, sl] = y * gv_g_ref[:, sl]

    for c in range(IN_BLOCKS):
        @pl.when(j == c)
        def _():
            epilogue(c, _dot(xn_ref[...], w_ref[...].astype(BF16)))


def _even_in(x_prompt, x_sample, g, w, ca, cb, gv_g):
    row = lambda i, j: (i, 0)
    const = lambda i, j: (0, 0)
    ahead = lambda i, j: (jnp.minimum(i + (j > IN_Q).astype(jnp.int32), N_TILES - 1), 0)
    done = lambda j, first, count: jnp.clip(j - first, 0, count - 1)
    return pl.pallas_call(
        _even_in_kernel,
        grid=(N_TILES, IN_BLOCKS),
        in_specs=[
            pl.BlockSpec((TMB, D_MODEL), ahead),
            pl.BlockSpec((N_SAMPLE, D_MODEL), const),
            pl.BlockSpec((1, D_MODEL), const),
            pl.BlockSpec((D_MODEL, IN_BLOCK), lambda i, j: (0, jnp.minimum(j, IN_BLOCKS - 1))),
            pl.BlockSpec((TMB, LANES), ahead),
            pl.BlockSpec((TMB, LANES), ahead),
            pl.BlockSpec((1, IN_BLOCK), lambda i, j: (0, done(j, IN_V, IN_BLOCKS - IN_V))),
        ],
        out_specs=[
            pl.BlockSpec((TMB, IN_BLOCK), lambda i, j: (i, done(j, 0, IN_Q))),
            pl.BlockSpec((TMB, KV_WIDTH), row),
            pl.BlockSpec((TMB, KV_WIDTH), row),
            pl.BlockSpec((TMB, IN_BLOCK), lambda i, j: (i, done(j, IN_U, IN_V - IN_U))),
            pl.BlockSpec((TMB, IN_BLOCK), lambda i, j: (i, done(j, IN_V, IN_BLOCKS - IN_V))),
        ],
        out_shape=[
            jax.ShapeDtypeStruct((N_TOK, A_WIDTH), BF16),
            jax.ShapeDtypeStruct((N_TOK, KV_WIDTH), F32),
            jax.ShapeDtypeStruct((N_TOK, KV_WIDTH), F32),
            jax.ShapeDtypeStruct((N_TOK, B_WIDTH), BF16),
            jax.ShapeDtypeStruct((N_TOK, B_WIDTH), F32),
        ],
        scratch_shapes=[pltpu.VMEM((TMB, D_MODEL), BF16)],
        compiler_params=_params(("arbitrary", "arbitrary")),
        name="even_in",
    )(x_prompt, x_sample, g, w, ca, cb, gv_g)


N_MIX_PROMPT = N_PROMPT // MIX_TM
ITEMS_PER_TILE = MIX_TM // DEC_SEQ
N_MIX = N_TOK // MIX_TM


def _spread_heads(x, kv_head):
    t = x[:, (kv_head // 2) * LANES:(kv_head // 2 + 1) * LANES]
    lane = lax.broadcasted_iota(jnp.int32, t.shape, 1)
    if kv_head % 2 == 0:
        lo = jnp.where(lane < HEAD_DIM, t, 0.0)
        hi = pltpu.roll(lo, HEAD_DIM, 1)
    else:
        hi = jnp.where(lane >= HEAD_DIM, t, 0.0)
        lo = pltpu.roll(hi, HEAD_DIM, 1)
    return lo, hi


def _kv_operands(kk, vv):
    s_len = kk[0].shape[0]
    kx = jnp.concatenate(kk, axis=0).astype(BF16)
    vx = jnp.concatenate(vv, axis=0)
    row = lax.broadcasted_iota(jnp.int32, vx.shape, 0)
    lane = lax.broadcasted_iota(jnp.int32, vx.shape, 1)
    own = ((row < s_len) & (lane < HEAD_DIM)) | ((row >= s_len) & (lane >= HEAD_DIM))
    ones = jnp.where(own, 1.0, 0.0)
    return kx, jnp.concatenate([vx, ones], axis=1).astype(BF16)


def _attend(q2, kx, vx1, sink_a, sink_b, allowed):
    s_len = kx.shape[0] // 2
    s = lax.dot_general(q2, kx, _NT, preferred_element_type=F32)
    ps, sink_terms = [], []
    for half, sink in ((0, sink_a), (1, sink_b)):
        sh = s[:, half * s_len:(half + 1) * s_len]
        if allowed is not None:
            sh = jnp.where(allowed, sh, -jnp.inf)
        m = jnp.maximum(jnp.max(sh, axis=-1, keepdims=True), sink)
        ps.append(jnp.exp(sh - m))
        sink_terms.append(jnp.exp(sink - m))
    o = _dot(jnp.concatenate(ps, axis=1).astype(BF16), vx1)
    lane = lax.broadcasted_iota(jnp.int32, (o.shape[0], LANES), 1)
    return o[:, :LANES] / (o[:, LANES:] + jnp.where(lane < HEAD_DIM, sink_terms[0], sink_terms[1]))


def _mixer_kernel(sink_ref, q_ref, kp_ref, kc_ref, vp_ref, vc_ref, ck_ref, cv_ref,
                  u_ref, gv_ref, wp_ref, ws_ref, bp_ref, bs_ref,
                  ab_ref, ko_ref, vo_ref):
    i = pl.program_id(0)

    def gmlp(w_ref, b_ref):
        for g in range(B_GROUPS):
            sl = slice(g * GMLP_DIM, (g + 1) * GMLP_DIM)
            mixed = _dot(w_ref[g], gv_ref[:, sl].astype(BF16)) + b_ref[:, g:g + 1]
            ab_ref[:, A_WIDTH + g * GMLP_DIM:A_WIDTH + (g + 1) * GMLP_DIM] = (
                u_ref[:, sl].astype(F32) * mixed).astype(BF16)

    @pl.when(i < N_MIX_PROMPT)
    def _prompt():
        kk = jnp.concatenate([kp_ref[...], kc_ref[...]], axis=0)
        vv = jnp.concatenate([vp_ref[...], vc_ref[...]], axis=0)
        row = lax.broadcasted_iota(jnp.int32, (MIX_TM, 2 * MIX_TM), 0)
        col = lax.broadcasted_iota(jnp.int32, (MIX_TM, 2 * MIX_TM), 1)
        allowed = ((row < CHUNK) & (col < 3 * CHUNK)) | ((row >= CHUNK) & (col >= CHUNK))
        allowed = allowed & ((col >= MIX_TM) | (i > 0))
        for j in range(A_KV_HEADS):
            kx, vx1 = _kv_operands(_spread_heads(kk, j), _spread_heads(vv, j))
            for p in (2 * j, 2 * j + 1):
                sl = slice(p * LANES, (p + 1) * LANES)
                o = _attend(q_ref[:, sl], kx, vx1, sink_ref[2 * p], sink_ref[2 * p + 1], allowed)
                ab_ref[:, sl] = o.astype(BF16)
        gmlp(wp_ref, bp_ref)

    @pl.when(i >= N_MIX_PROMPT)
    def _sample():
        span = WINDOW + DEC_SEQ
        k_parts, v_parts = [], []
        for a in range(ITEMS_PER_TILE):
            rows = slice(a * DEC_SEQ, (a + 1) * DEC_SEQ)
            k_parts += [ck_ref[a], kc_ref[rows, :]]
            v_parts += [cv_ref[a], vc_ref[rows, :]]
            ko_ref[a] = jnp.concatenate([ck_ref[a, DEC_SEQ:, :], kc_ref[rows, :]], axis=0)
            vo_ref[a] = jnp.concatenate([cv_ref[a, DEC_SEQ:, :], vc_ref[rows, :]], axis=0)
        kk = jnp.concatenate(k_parts, axis=0)
        vv = jnp.concatenate(v_parts, axis=0)
        row = lax.broadcasted_iota(jnp.int32, (MIX_TM, ITEMS_PER_TILE * span), 0)
        col = lax.broadcasted_iota(jnp.int32, (MIX_TM, ITEMS_PER_TILE * span), 1)
        allowed = None
        for a in range(ITEMS_PER_TILE):
            own = ((row >= a * DEC_SEQ) & (row < (a + 1) * DEC_SEQ)
                   & (col >= a * span) & (col < (a + 1) * span))
            allowed = own if allowed is None else allowed | own
        for j in range(A_KV_HEADS):
            kx, vx1 = _kv_operands(_spread_heads(kk, j), _spread_heads(vv, j))
            for p in (2 * j, 2 * j + 1):
                sl = slice(p * LANES, (p + 1) * LANES)
                o = _attend(q_ref[:, sl], kx, vx1, sink_ref[2 * p], sink_ref[2 * p + 1], allowed)
                ab_ref[:, sl] = o.astype(BF16)
        gmlp(ws_ref, bs_ref)


def _mixer(sinks, q, k, v, cache_k, cache_v, u, gv, w_p, w_s, b_p, b_s):
    row = lambda i: (i, 0)
    prev = lambda i: (jnp.maximum(i - 1, 0), 0)
    item = lambda i: (jnp.maximum(i - N_MIX_PROMPT, 0), 0, 0)
    c2 = lambda i: (0, 0)
    c3 = lambda i: (0, 0, 0)
    cache_block = (ITEMS_PER_TILE, WINDOW, KV_WIDTH)
    return pl.pallas_call(
        _mixer_kernel,
        grid=(N_MIX,),
        in_specs=[
            pl.BlockSpec(memory_space=pltpu.SMEM),
            pl.BlockSpec((MIX_TM, A_WIDTH), row),
            pl.BlockSpec((MIX_TM, KV_WIDTH), prev),
            pl.BlockSpec((MIX_TM, KV_WIDTH), row),
            pl.BlockSpec((MIX_TM, KV_WIDTH), prev),
            pl.BlockSpec((MIX_TM, KV_WIDTH), row),
            pl.BlockSpec(cache_block, item),
            pl.BlockSpec(cache_block, item),
            pl.BlockSpec((MIX_TM, B_WIDTH), row),
            pl.BlockSpec((MIX_TM, B_WIDTH), row),
            pl.BlockSpec((B_GROUPS, MIX_TM, MIX_TM), c3),
            pl.BlockSpec((B_GROUPS, MIX_TM, MIX_TM), c3),
            pl.BlockSpec((MIX_TM, B_GROUPS), c2),
            pl.BlockSpec((MIX_TM, B_GROUPS), c2),
        ],
        out_specs=[
            pl.BlockSpec((MIX_TM, A_WIDTH + B_WIDTH), row),
            pl.BlockSpec(cache_block, item),
            pl.BlockSpec(cache_block, item),
        ],
        out_shape=[
            jax.ShapeDtypeStruct((N_TOK, A_WIDTH + B_WIDTH), BF16),
            jax.ShapeDtypeStruct((DEC_BATCH, WINDOW, KV_WIDTH), F32),
            jax.ShapeDtypeStruct((DEC_BATCH, WINDOW, KV_WIDTH), F32),
        ],
        compiler_params=_params(("arbitrary",)),
        name="mixer",
    )(sinks, q, k, k, v, v, cache_k, cache_v, u, gv, w_p, w_s, b_p, b_s)


N_PJ = D_MODEL // PJ_BLOCK
N_FF = D_FF // FF_BLOCK


def _proj_ffn_kernel(*refs, split_in, final):
    refs = list(refs)
    h_refs = [refs.pop(0) for _ in range(2 if split_in else 1)]
    y_ref, wo_ref, gf_ref, wu_ref, wd_ref, gn_ref, o_ref = refs[:7]
    if final:
        ys_ref, xn_ref = refs[7:]
    else:
        (xn_ref,) = refs[7:]
    i = pl.program_id(0)
    s = pl.program_id(1)
    last_tile = i == N_TILES - 1
    cols = [slice(c * MXU_N, (c + 1) * MXU_N) for c in range(D_MODEL // MXU_N)]

    def accumulate(lhs, w, first=False):
        for sl in cols:
            if first:
                o_ref[:, sl] = _dot(lhs, w[:, sl])
            else:
                o_ref[:, sl] += _dot(lhs, w[:, sl])

    @pl.when(s == 0)
    def _():
        accumulate(y_ref[...], wo_ref[...].astype(BF16), first=True)

    @pl.when((s > 0) & (s < N_PJ))
    def _():
        accumulate(y_ref[...], wo_ref[...].astype(BF16))

    for c in range(N_PJ):
        sl = slice(c * PJ_BLOCK, (c + 1) * PJ_BLOCK)
        if split_in:
            hp_ref, hs_ref = h_refs

            @pl.when((s == c) & jnp.logical_not(last_tile))
            def _():
                o_ref[:, sl] += hp_ref[...]

            @pl.when((s == c) & last_tile)
            def _():
                o_ref[0:TAIL, sl] += hp_ref[0:TAIL, :]
                o_ref[TAIL:, sl] += hs_ref[...]
        else:
            @pl.when(s == c)
            def _():
                o_ref[:, sl] += h_refs[0][...]

    @pl.when(s == N_PJ)
    def _():
        for a, b in _row_spans(0, TMB):
            xn_ref[a:b, :] = _rms(o_ref[a:b, :], gf_ref[...]).astype(BF16)

    @pl.when(s >= N_PJ)
    def _():
        t = jnp.maximum(_dot(xn_ref[...], wu_ref[...].astype(BF16)), 0.0)
        accumulate((t * t).astype(BF16), wd_ref[...].astype(BF16))

    last_step = s == N_PJ + N_FF - 1
    if final:
        @pl.when(last_step & jnp.logical_not(last_tile))
        def _():
            for a, b in _row_spans(0, TMB):
                o_ref[a:b, :] = _rms(o_ref[a:b, :], gn_ref[...])

        @pl.when(last_step & last_tile)
        def _():
            for a, b in _row_spans(0, TAIL):
                o_ref[a:b, :] = _rms(o_ref[a:b, :], gn_ref[...])
            for a, b in _row_spans(0, N_SAMPLE):
                ys_ref[a:b, :] = _rms(o_ref[TAIL + a:TAIL + b, :], gn_ref[...])
    else:
        @pl.when(last_step)
        def _():
            for a, b in _row_spans(0, TMB):
                xn_ref[a:b, :] = _rms(o_ref[a:b, :], gn_ref[...]).astype(BF16)


def _proj_ffn(h, y, w_out, g_ffn, w_up, w_down, layer, g_next, final):
    split_in = isinstance(h, tuple)
    row = lambda i, s: (i, 0)
    const = lambda i, s: (0, 0)
    pj = lambda s: jnp.minimum(s, N_PJ - 1)
    ff = lambda s: jnp.maximum(s - N_PJ, 0)
    h_specs = [pl.BlockSpec((TMB, PJ_BLOCK), lambda i, s: (i, pj(s)))]
    if split_in:
        h_specs.append(pl.BlockSpec((N_SAMPLE, PJ_BLOCK), lambda i, s: (0, pj(s))))
    if final:
        out_specs = [pl.BlockSpec((TMB, D_MODEL), row), pl.BlockSpec((N_SAMPLE, D_MODEL), const)]
        out_shape = [jax.ShapeDtypeStruct((N_PROMPT, D_MODEL), F32),
                     jax.ShapeDtypeStruct((N_SAMPLE, D_MODEL), F32)]
        scratch = [pltpu.VMEM((TMB, D_MODEL), BF16)]
    else:
        out_specs = [pl.BlockSpec((TMB, D_MODEL), row), pl.BlockSpec((TMB, D_MODEL), row)]
        out_shape = [jax.ShapeDtypeStruct((N_TOK, D_MODEL), F32),
                     jax.ShapeDtypeStruct((N_TOK, D_MODEL), BF16)]
        scratch = []
    return pl.pallas_call(
        functools.partial(_proj_ffn_kernel, split_in=split_in, final=final),
        grid=(N_TILES, N_PJ + N_FF),
        in_specs=[
            *h_specs,
            pl.BlockSpec((TMB, PJ_BLOCK), lambda i, s: (i, pj(s))),
            pl.BlockSpec((PJ_BLOCK, D_MODEL), lambda i, s: (pj(s), 0)),
            pl.BlockSpec((1, D_MODEL), const),
            pl.BlockSpec((None, D_MODEL, FF_BLOCK), lambda i, s: (layer, 0, ff(s))),
            pl.BlockSpec((None, FF_BLOCK, D_MODEL), lambda i, s: (layer, ff(s), 0)),
            pl.BlockSpec((1, D_MODEL), const),
        ],
        out_specs=out_specs,
        out_shape=out_shape,
        scratch_shapes=scratch,
        compiler_params=_params(("arbitrary", "arbitrary")),
        name="proj_ffn_final" if final else "proj_ffn",
    )(*(h if split_in else (h,)), y, w_out, g_ffn, w_up, w_down, g_next)


N_HG_PROMPT = N_PROMPT // TM
N_HG = N_HG_PROMPT + 1
CHUNKS_PER_TILE = TM // HG_CHUNK
SAMPLE_CHUNKS = N_SAMPLE // HG_CHUNK


def _hgrn_scores(q, k, f, b_ref, r0, seg):
    n = HG_CHUNK
    sub = 8
    apart = [(lax.broadcasted_iota(jnp.int32, (sub, n), 0) + r * sub)
             ^ lax.broadcasted_iota(jnp.int32, (sub, n), 1) for r in range(n // sub)]
    scores = [None] * (n // sub)

    def merge(r, h, block):
        if scores[r] is None:
            scores[r] = block if 2 * h == n else jnp.where(apart[r] < 2 * h, block, 0.0)
        else:
            scores[r] = jnp.where(apart[r] < 2 * h, block, scores[r])

    def coarse(h):
        qs, ks = [], []
        zero = jnp.zeros((h, C_DIM), F32)
        for g in range(n // (2 * h)):
            base = r0 + g * 2 * h
            beta = b_ref[base + h - 1:base + h, :]
            lo = slice(g * 2 * h, g * 2 * h + h)
            up = slice(g * 2 * h + h, (g + 1) * 2 * h)
            ks += [k[lo] * jnp.exp2(beta - b_ref[base:base + h, :]), zero]
            qs.append(q[up] * jnp.exp2(b_ref[base + h:base + 2 * h, :] - beta))
        return jnp.concatenate(qs, axis=0).astype(BF16), jnp.concatenate(ks, axis=0).astype(BF16)

    def fine(h):
        if h == 4:
            qs, ks = [], []
            upper = lax.broadcasted_iota(jnp.int32, (8, C_DIM), 0) >= 4
            for g in range(n // 8):
                sl = slice(8 * g, 8 * g + 8)
                base = r0 + 8 * g
                e = jnp.exp2(-jnp.abs(b_ref[base:base + 8, :] - b_ref[base + 3:base + 4, :]))
                qs.append(jnp.where(upper, q[sl] * e, 0.0))
                ks.append(jnp.where(upper, 0.0, k[sl] * e))
            qt = jnp.concatenate(qs, axis=0)
            kt = jnp.concatenate(ks, axis=0)
        elif h == 2:
            r4 = lax.broadcasted_iota(jnp.int32, (n, C_DIM), 0) & 3
            f_next = pltpu.roll(f, n - 1, 0)
            f_prev = pltpu.roll(f, 1, 0)
            kt = jnp.where(r4 == 0, k * f_next, jnp.where(r4 == 1, k, 0.0))
            qt = jnp.where(r4 == 2, q * f, jnp.where(r4 == 3, q * f * f_prev, 0.0))
        else:
            odd = (lax.broadcasted_iota(jnp.int32, (n, C_DIM), 0) & 1) == 1
            kt = jnp.where(odd, 0.0, k)
            qt = jnp.where(odd, q * f, 0.0)
        return qt.astype(BF16), kt.astype(BF16)

    h = seg // 2
    while h >= 1:
        qt, kt = coarse(h) if h >= 8 else fine(h)
        s_h = lax.dot_general(qt, kt, _NT, preferred_element_type=F32)
        if h >= 8:
            per_group = h // sub
            for i in range(n // 2 // sub):
                g, j = divmod(i, per_group)
                merge((g * 2 * h + h) // sub + j, h, s_h[i * sub:(i + 1) * sub])
        else:
            for r in range(n // sub):
                merge(r, h, s_h[r * sub:(r + 1) * sub])
        h //= 2
    diag = jnp.sum(q * k, axis=1, keepdims=True)
    blocks = []
    for r in range(n // sub):
        lower = jnp.zeros((sub, n), F32) if scores[r] is None else scores[r]
        blocks.append(jnp.where(apart[r] == 0, diag[r * sub:(r + 1) * sub], lower))
    return jnp.concatenate(blocks, axis=0)


def _hgrn_kernel(xn_ref, xn_s_ref, wq_ref, wf_ref, wi_ref, wg_ref, lb_ref, go_ref, trip_ref,
                 tris_ref, s0_ref, og_ref, sp_ref, ss_ref, w_ref, z_ref, f_ref, b_ref, st_ref):
    t = pl.program_id(1)

    lbx = lb_ref[...]
    e = jnp.exp(lbx - jnp.max(lbx, axis=0, keepdims=True))
    p = e / jnp.sum(e, axis=0, keepdims=True)
    lb = (p[0:1, :] + p[1:2, :]) - p[0:1, :]

    @pl.when(t == 0)
    def _():
        st_ref[...] = jnp.zeros_like(st_ref)
        for c, src in enumerate((wq_ref, wf_ref, wi_ref, wg_ref)):
            w_ref[:, c * C_DIM:(c + 1) * C_DIM] = src[...].astype(BF16)

    def cumulative_log_decay(tri_ref, n_chunks):
        lf = jnp.log2(f_ref[0:n_chunks * HG_CHUNK, :])
        lf = jnp.concatenate([lf[c * HG_CHUNK:(c + 1) * HG_CHUNK] for c in range(n_chunks)], axis=1)
        hi = lf.astype(BF16)
        lo = (lf - hi.astype(F32)).astype(BF16)
        bb = _dot(tri_ref[...], jnp.concatenate([hi, lo], axis=1))
        for c in range(n_chunks):
            lanes = slice(c * C_DIM, (c + 1) * C_DIM)
            b_ref[c * HG_CHUNK:(c + 1) * HG_CHUNK, :] = bb[:, lanes] + bb[:, n_chunks * C_DIM:][:, lanes]

    def chunk(c, seg):
        r0 = c * HG_CHUNK
        rows = slice(r0, r0 + HG_CHUNK)
        zq = z_ref[rows, 0:C_DIM]
        f = f_ref[rows, :]
        v = z_ref[rows, 2 * C_DIM:3 * C_DIM]
        gate = _sigmoid(z_ref[rows, 3 * C_DIM:4 * C_DIM])
        q = zq * _sigmoid(zq)
        k = 1.0 - f
        scores = _hgrn_scores(q, k, f, b_ref, r0, seg)
        o = _dot(scores.astype(BF16), v.astype(BF16))
        b = b_ref[rows, :]
        qe = (q * jnp.exp2(b)).astype(BF16)
        if seg == HG_CHUNK:
            s_t = st_ref[...]
            o = o + lax.dot_general(qe, s_t.astype(BF16), _NT, preferred_element_type=F32)
            b_last = b_ref[r0 + HG_CHUNK - 1:r0 + HG_CHUNK, :]
            ke = (k * jnp.exp2(b_last - b)).astype(BF16)
            st_ref[...] = s_t * jnp.exp2(b_last) + _dot(v.T.astype(BF16), ke)
        else:
            outs = []
            v_t = v.T.astype(BF16)
            row = lax.broadcasted_iota(jnp.int32, (HG_CHUNK, C_DIM), 0)
            for a in range(HG_CHUNK // seg):
                item = c * (HG_CHUNK // seg) + a
                r = slice(a * seg, (a + 1) * seg)
                s_t = s0_ref[item].T
                outs.append(lax.dot_general(qe[r], s_t.astype(BF16), _NT, preferred_element_type=F32))
                b_last = b_ref[r0 + (a + 1) * seg - 1:r0 + (a + 1) * seg, :]
                own = (row >= a * seg) & (row < (a + 1) * seg)
                ke = jnp.where(own, k * jnp.exp2(jnp.minimum(b_last - b, 0.0)), 0.0).astype(BF16)
                s_new = s_t * jnp.exp2(b_last) + _dot(v_t, ke)
                ss_ref[item] = s_new.T
            o = o + jnp.concatenate(outs, axis=0)
        y = _rms(o, go_ref[...]) * gate
        og_ref[rows, :] = y.astype(BF16)

    def tile(src_ref, tri_ref, seg, n_chunks):
        rows = slice(0, n_chunks * HG_CHUNK)
        z_ref[rows, :] = _dot(src_ref[...], w_ref[...])
        f_ref[rows, :] = lb + (1.0 - lb) * _sigmoid(z_ref[rows, C_DIM:2 * C_DIM])
        cumulative_log_decay(tri_ref, n_chunks)
        for c in range(n_chunks):
            chunk(c, seg)

    @pl.when(t < N_HG_PROMPT)
    def _():
        tile(xn_ref, trip_ref, HG_CHUNK, CHUNKS_PER_TILE)

    @pl.when(t == N_HG_PROMPT - 1)
    def _():
        sp_ref[...] = st_ref[...].T

    @pl.when(t == N_HG_PROMPT)
    def _():
        tile(xn_s_ref, tris_ref, DEC_SEQ, SAMPLE_CHUNKS)


def _hgrn(xn, w_in, lb, g_onorm, tri_p, tri_s, state0):
    c2 = lambda j, t: (0, 0)
    head_col = lambda j, t: (0, j)
    w_specs = [pl.BlockSpec((D_MODEL, C_DIM), functools.partial(lambda j, t, c: (0, c * C_HEADS + j), c=c))
               for c in range(4)]
    return pl.pallas_call(
        _hgrn_kernel,
        grid=(C_HEADS, N_HG),
        in_specs=[
            pl.BlockSpec((TM, D_MODEL), lambda j, t: (jnp.minimum(t, N_HG_PROMPT - 1), 0)),
            pl.BlockSpec((N_SAMPLE, D_MODEL), lambda j, t: (N_PROMPT // N_SAMPLE, 0)),
            *w_specs,
            pl.BlockSpec((2, C_DIM), head_col),
            pl.BlockSpec((1, C_DIM), head_col),
            pl.BlockSpec((HG_CHUNK, HG_CHUNK), c2),
            pl.BlockSpec((HG_CHUNK, HG_CHUNK), c2),
            pl.BlockSpec((DEC_BATCH, None, C_DIM, C_DIM), lambda j, t: (0, j, 0, 0)),
        ],
        out_specs=[
            pl.BlockSpec((TM, C_DIM), lambda j, t: (t, j)),
            pl.BlockSpec((None, C_DIM, C_DIM), lambda j, t: (j, 0, 0)),
            pl.BlockSpec((DEC_BATCH, None, C_DIM, C_DIM), lambda j, t: (0, j, 0, 0)),
        ],
        out_shape=[
            jax.ShapeDtypeStruct((N_TOK, C_HEADS * C_DIM), BF16),
            jax.ShapeDtypeStruct((C_HEADS, C_DIM, C_DIM), F32),
            jax.ShapeDtypeStruct((DEC_BATCH, C_HEADS, C_DIM, C_DIM), F32),
        ],
        scratch_shapes=[
            pltpu.VMEM((D_MODEL, 4 * C_DIM), BF16),
            pltpu.VMEM((TM, 4 * C_DIM), F32),
            pltpu.VMEM((TM, C_DIM), F32),
            pltpu.VMEM((TM, C_DIM), F32),
            pltpu.VMEM((C_DIM, C_DIM), F32),
        ],
        compiler_params=_params(("arbitrary", "arbitrary")),
        name="hgrn",
    )(xn, xn, w_in, w_in, w_in, w_in, lb, g_onorm, tri_p, tri_s, state0)


def _rope_tables():
    half = ROT_DIM // 2
    inv = ROPE_THETA ** (-(jnp.arange(half, dtype=F32) * (2.0 / ROT_DIM)))
    pos = jnp.concatenate([jnp.arange(N_PROMPT, dtype=jnp.int32),
                           jnp.tile(PAST_LEN + jnp.arange(DEC_SEQ, dtype=jnp.int32), DEC_BATCH)])
    ang = pos.astype(F32)[:, None] * inv[None, :]
    cos, sin = jnp.cos(ang), jnp.sin(ang)
    ones = jnp.ones((N_TOK, HEAD_DIM - ROT_DIM), F32)
    ca = jnp.concatenate([cos, cos, ones], axis=1)
    cb = jnp.concatenate([-sin, sin, 0.0 * ones], axis=1)
    return jnp.tile(ca, (1, LANES // HEAD_DIM)), jnp.tile(cb, (1, LANES // HEAD_DIM))


def kernel(x_prompt, x_sample, cache_swa_k, cache_swa_v, state_hgrn, norm_mix, norm_ffn, norm_final,
           w_in_even, w_out_even, attn_sinks, gmlp_vnorm, gmlp_ws, gmlp_bias, w_in_odd, w_out_odd,
           hgrn_lb, hgrn_onorm, w_ffn_up, w_ffn_down):
    xp = x_prompt.reshape(N_PROMPT, D_MODEL)
    xs = x_sample.reshape(N_SAMPLE, D_MODEL)
    ca, cb = _rope_tables()

    q, k, v, u, gv = _even_in(xp, xs, norm_mix[0:1], w_in_even[0], ca, cb, gmlp_vnorm[0:1])
    ws = gmlp_ws[0]
    cidx = jnp.arange(MIX_TM) // CHUNK
    w_p = jnp.where(cidx[:, None] >= cidx[None, :], ws, 0.0).astype(BF16)
    eye = jnp.eye(ITEMS_PER_TILE, dtype=F32)
    w_s = jnp.einsum('ab,gpq->gapbq', eye, ws[:, :DEC_SEQ, :DEC_SEQ]).reshape(B_GROUPS, MIX_TM, MIX_TM).astype(BF16)
    b_p = gmlp_bias[0].T
    b_s = jnp.tile(gmlp_bias[0][:, :DEC_SEQ], (1, ITEMS_PER_TILE)).T
    ab, k_s, v_s = _mixer(attn_sinks[0], q, k, v,
                          cache_swa_k[0].reshape(DEC_BATCH, WINDOW, KV_WIDTH),
                          cache_swa_v[0].reshape(DEC_BATCH, WINDOW, KV_WIDTH),
                          u, gv, w_p, w_s, b_p, b_s)
    h1, xn1 = _proj_ffn((xp, xs), ab, w_out_even[0], norm_ffn[0:1], w_ffn_up, w_ffn_down, 0,
                        norm_mix[1:2], final=False)

    tri_p = jnp.tril(jnp.ones((HG_CHUNK, HG_CHUNK), F32))
    seg = jnp.arange(HG_CHUNK) // DEC_SEQ
    tri_s = jnp.where(seg[:, None] == seg[None, :], tri_p, 0.0)
    og, s_p, s_s = _hgrn(xn1, w_in_odd[0], hgrn_lb, hgrn_onorm[0:1], tri_p.astype(BF16), tri_s.astype(BF16),
                         state_hgrn[0])
    y_p, y_s = _proj_ffn(h1, og, w_out_odd[0], norm_ffn[1:2], w_ffn_up, w_ffn_down, 1,
                         norm_final[None, :], final=True)

    kv_shape = (1, 1, WINDOW, A_KV_HEADS, HEAD_DIM)
    kv_s_shape = (1, DEC_BATCH, WINDOW, A_KV_HEADS, HEAD_DIM)
    return (y_p.reshape(1, N_PROMPT, D_MODEL),
            y_s.reshape(DEC_BATCH, DEC_SEQ, D_MODEL),
            k[N_PROMPT - WINDOW:N_PROMPT].reshape(kv_shape),
            v[N_PROMPT - WINDOW:N_PROMPT].reshape(kv_shape),
            k_s.reshape(kv_s_shape),
            v_s.reshape(kv_s_shape),
            gv[N_PROMPT:].reshape(1, DEC_BATCH, DEC_SEQ, B_GROUPS, GMLP_DIM),
            s_p[None, None],
            s_s[None])
```

```python
import functools
import math

import jax
import jax.numpy as jnp
from jax import lax
from jax.experimental import pallas as pl
from jax.experimental.pallas import tpu as pltpu

F32 = jnp.float32
BF16 = jnp.bfloat16

D_MODEL = 2048
N_PROMPT = 8192
DEC_BATCH = 16
DEC_SEQ = 32
N_SAMPLE = DEC_BATCH * DEC_SEQ
N_TOK = N_PROMPT + N_SAMPLE
PAST_LEN = 2048
EPS = 1e-6

HEAD_DIM = 64
A_HEADS = 16
A_KV_HEADS = 4
A_WIDTH = A_HEADS * HEAD_DIM
KV_WIDTH = A_KV_HEADS * HEAD_DIM
ROT_DIM = 16
ROPE_THETA = 500000.0
WINDOW = 128
CHUNK = 64
B_GROUPS = 8
GMLP_DIM = 128
B_WIDTH = B_GROUPS * GMLP_DIM
EVEN_IN = A_WIDTH + 2 * KV_WIDTH + 2 * B_WIDTH
C_HEADS = 16
C_DIM = 128
D_FF = 4 * D_MODEL

LANES = 128
TM = 2048
TMB = 1088
ROW_CHUNK = 272
N_TILES = N_TOK // TMB
TAIL = N_PROMPT - (N_TILES - 1) * TMB
MIX_TM = 128
HG_CHUNK = 128
IN_BLOCK = 512
PJ_BLOCK = 512
FF_BLOCK = 512
MXU_N = 256
VMEM_LIMIT = 61 * 1024 * 1024

_NT = (((1,), (1,)), ((), ()))


def _params(sem):
    return pltpu.CompilerParams(dimension_semantics=sem, vmem_limit_bytes=VMEM_LIMIT)


def _rms(x, g):
    return x * lax.rsqrt(jnp.mean(x * x, axis=-1, keepdims=True) + EPS) * g


def _sigmoid(x):
    return 0.5 * jnp.tanh(0.5 * x) + 0.5


def _gelu(x):
    c = math.sqrt(2.0 / math.pi)
    return x * (0.5 * (1.0 + jnp.tanh(c * (x + 0.044715 * (x * x * x)))))


def _dot(a, b):
    return jnp.dot(a, b, preferred_element_type=F32)


IN_BLOCKS = EVEN_IN // IN_BLOCK
IN_Q = A_WIDTH // IN_BLOCK
IN_U = IN_Q + 1
IN_V = IN_U + B_WIDTH // IN_BLOCK


def _row_spans(start, stop, step=ROW_CHUNK):
    return [(r, min(r + step, stop)) for r in range(start, stop, step)]


def _even_in_kernel(xp_ref, xs_ref, g_ref, w_ref, ca_ref, cb_ref, gv_g_ref,
                    q_ref, k_ref, v_ref, u_ref, gv_ref, xn_ref):
    i = pl.program_id(0)
    j = pl.program_id(1)

    def norm_rows(src_ref, src0, dst0, n):
        for a, b in _row_spans(0, n):
            xn_ref[dst0 + a:dst0 + b, :] = _rms(src_ref[src0 + a:src0 + b, :], g_ref[...]).astype(BF16)

    @pl.when((j == 0) & (i < N_TILES - 1))
    def _():
        norm_rows(xp_ref, 0, 0, TMB)

    @pl.when((j == 0) & (i == N_TILES - 1))
    def _():
        norm_rows(xp_ref, 0, 0, TAIL)
        norm_rows(xs_ref, 0, TAIL, N_SAMPLE)

    def rope(x):
        lane = lax.broadcasted_iota(jnp.int32, (TMB, LANES), 1)
        first = (lane & (HEAD_DIM - 1)) < (ROT_DIM // 2)
        partner = jnp.where(first, pltpu.roll(x, LANES - ROT_DIM // 2, 1), pltpu.roll(x, ROT_DIM // 2, 1))
        return x * ca_ref[...] + partner * cb_ref[...]

    tiles = [slice(t * LANES, (t + 1) * LANES) for t in range(IN_BLOCK // LANES)]

    def epilogue(c, z):
        if c < IN_Q:
            for sl in tiles:
                q_ref[:, sl] = (rope(z[:, sl]) * (HEAD_DIM ** -0.5)).astype(BF16)
        elif c == IN_Q:
            for sl in tiles[:KV_WIDTH // LANES]:
                k_ref[:, sl] = rope(z[:, sl])
            v_ref[...] = z[:, KV_WIDTH:]
        elif c < IN_V:
            u_ref[...] = _gelu(z).astype(BF16)
        else:
            for sl in tiles:
                x = _gelu(z[:, sl])
                xc = x - jnp.mean(x, axis=-1, keepdims=True)
                y = xc * lax.rsqrt(jnp.mean(xc * xc, axis=-1, keepdims=True) + EPS)
                gv_ref[:, sl] = y * gv_g_ref[:, sl]

    for c in range(IN_BLOCKS):
        @pl.when(j == c)
        def _():
            epilogue(c, _dot(xn_ref[...], w_ref[...].astype(BF16)))


def _even_in(x_prompt, x_sample, g, w, ca, cb, gv_g):
    row = lambda i, j: (i, 0)
    const = lambda i, j: (0, 0)
    ahead = lambda i, j: (jnp.minimum(i + (j > IN_Q).astype(jnp.int32), N_TILES - 1), 0)
    done = lambda j, first, count: jnp.clip(j - first, 0, count - 1)
    return pl.pallas_call(
        _even_in_kernel,
        grid=(N_TILES, IN_BLOCKS),
        in_specs=[
            pl.BlockSpec((TMB, D_MODEL), ahead),
            pl.BlockSpec((N_SAMPLE, D_MODEL), const),
            pl.BlockSpec((1, D_MODEL), const),
            pl.BlockSpec((D_MODEL, IN_BLOCK), lambda i, j: (0, jnp.minimum(j, IN_BLOCKS - 1))),
            pl.BlockSpec((TMB, LANES), ahead),
            pl.BlockSpec((TMB, LANES), ahead),
            pl.BlockSpec((1, IN_BLOCK), lambda i, j: (0, done(j, IN_V, IN_BLOCKS - IN_V))),
        ],
        out_specs=[
            pl.BlockSpec((TMB, IN_BLOCK), lambda i, j: (i, done(j, 0, IN_Q))),
            pl.BlockSpec((TMB, KV_WIDTH), row),
            pl.BlockSpec((TMB, KV_WIDTH), row),
            pl.BlockSpec((TMB, IN_BLOCK), lambda i, j: (i, done(j, IN_U, IN_V - IN_U))),
            pl.BlockSpec((TMB, IN_BLOCK), lambda i, j: (i, done(j, IN_V, IN_BLOCKS - IN_V))),
        ],
        out_shape=[
            jax.ShapeDtypeStruct((N_TOK, A_WIDTH), BF16),
            jax.ShapeDtypeStruct((N_TOK, KV_WIDTH), F32),
            jax.ShapeDtypeStruct((N_TOK, KV_WIDTH), F32),
            jax.ShapeDtypeStruct((N_TOK, B_WIDTH), BF16),
            jax.ShapeDtypeStruct((N_TOK, B_WIDTH), F32),
        ],
        scratch_shapes=[pltpu.VMEM((TMB, D_MODEL), BF16)],
        compiler_params=_params(("arbitrary", "arbitrary")),
        name="even_in",
    )(x_prompt, x_sample, g, w, ca, cb, gv_g)


N_MIX_PROMPT = N_PROMPT // MIX_TM
ITEMS_PER_TILE = MIX_TM // DEC_SEQ
N_MIX = N_TOK // MIX_TM


def _spread_heads(x, kv_head):
    t = x[:, (kv_head // 2) * LANES:(kv_head // 2 + 1) * LANES]
    lane = lax.broadcasted_iota(jnp.int32, t.shape, 1)
    if kv_head % 2 == 0:
        lo = jnp.where(lane < HEAD_DIM, t, 0.0)
        hi = pltpu.roll(lo, HEAD_DIM, 1)
    else:
        hi = jnp.where(lane >= HEAD_DIM, t, 0.0)
        lo = pltpu.roll(hi, HEAD_DIM, 1)
    return lo, hi


def _kv_operands(kk, vv):
    s_len = kk[0].shape[0]
    kx = jnp.concatenate(kk, axis=0).astype(BF16)
    vx = jnp.concatenate(vv, axis=0)
    row = lax.broadcasted_iota(jnp.int32, vx.shape, 0)
    lane = lax.broadcasted_iota(jnp.int32, vx.shape, 1)
    own = ((row < s_len) & (lane < HEAD_DIM)) | ((row >= s_len) & (lane >= HEAD_DIM))
    ones = jnp.where(own, 1.0, 0.0)
    return kx, jnp.concatenate([vx, ones], axis=1).astype(BF16)


def _attend(q2, kx, vx1, sink_a, sink_b, allowed):
    s_len = kx.shape[0] // 2
    s = lax.dot_general(q2, kx, _NT, preferred_element_type=F32)
    ps, sink_terms = [], []
    for half, sink in ((0, sink_a), (1, sink_b)):
        sh = s[:, half * s_len:(half + 1) * s_len]
        if allowed is not None:
            sh = jnp.where(allowed, sh, -jnp.inf)
        m = jnp.maximum(jnp.max(sh, axis=-1, keepdims=True), sink)
        ps.append(jnp.exp(sh - m))
        sink_terms.append(jnp.exp(sink - m))
    o = _dot(jnp.concatenate(ps, axis=1).astype(BF16), vx1)
    lane = lax.broadcasted_iota(jnp.int32, (o.shape[0], LANES), 1)
    return o[:, :LANES] / (o[:, LANES:] + jnp.where(lane < HEAD_DIM, sink_terms[0], sink_terms[1]))


def _mixer_kernel(sink_ref, q_ref, kp_ref, kc_ref, vp_ref, vc_ref, ck_ref, cv_ref,
                  u_ref, gv_ref, wp_ref, ws_ref, bp_ref, bs_ref,
                  ab_ref, ko_ref, vo_ref):
    i = pl.program_id(0)

    def gmlp(w_ref, b_ref):
        for g in range(B_GROUPS):
            sl = slice(g * GMLP_DIM, (g + 1) * GMLP_DIM)
            mixed = _dot(w_ref[g], gv_ref[:, sl].astype(BF16)) + b_ref[:, g:g + 1]
            ab_ref[:, A_WIDTH + g * GMLP_DIM:A_WIDTH + (g + 1) * GMLP_DIM] = (
                u_ref[:, sl].astype(F32) * mixed).astype(BF16)

    @pl.when(i < N_MIX_PROMPT)
    def _prompt():
        kk = jnp.concatenate([kp_ref[...], kc_ref[...]], axis=0)
        vv = jnp.concatenate([vp_ref[...], vc_ref[...]], axis=0)
        row = lax.broadcasted_iota(jnp.int32, (MIX_TM, 2 * MIX_TM), 0)
        col = lax.broadcasted_iota(jnp.int32, (MIX_TM, 2 * MIX_TM), 1)
        allowed = ((row < CHUNK) & (col < 3 * CHUNK)) | ((row >= CHUNK) & (col >= CHUNK))
        allowed = allowed & ((col >= MIX_TM) | (i > 0))
        for j in range(A_KV_HEADS):
            kx, vx1 = _kv_operands(_spread_heads(kk, j), _spread_heads(vv, j))
            for p in (2 * j, 2 * j + 1):
                sl = slice(p * LANES, (p + 1) * LANES)
                o = _attend(q_ref[:, sl], kx, vx1, sink_ref[2 * p], sink_ref[2 * p + 1], allowed)
                ab_ref[:, sl] = o.astype(BF16)
        gmlp(wp_ref, bp_ref)

    @pl.when(i >= N_MIX_PROMPT)
    def _sample():
        span = WINDOW + DEC_SEQ
        k_parts, v_parts = [], []
        for a in range(ITEMS_PER_TILE):
            rows = slice(a * DEC_SEQ, (a + 1) * DEC_SEQ)
            k_parts += [ck_ref[a], kc_ref[rows, :]]
            v_parts += [cv_ref[a], vc_ref[rows, :]]
            ko_ref[a] = jnp.concatenate([ck_ref[a, DEC_SEQ:, :], kc_ref[rows, :]], axis=0)
            vo_ref[a] = jnp.concatenate([cv_ref[a, DEC_SEQ:, :], vc_ref[rows, :]], axis=0)
        kk = jnp.concatenate(k_parts, axis=0)
        vv = jnp.concatenate(v_parts, axis=0)
        row = lax.broadcasted_iota(jnp.int32, (MIX_TM, ITEMS_PER_TILE * span), 0)
        col = lax.broadcasted_iota(jnp.int32, (MIX_TM, ITEMS_PER_TILE * span), 1)
        allowed = None
        for a in range(ITEMS_PER_TILE):
            own = ((row >= a * DEC_SEQ) & (row < (a + 1) * DEC_SEQ)
                   & (col >= a * span) & (col < (a + 1) * span))
            allowed = own if allowed is None else allowed | own
        for j in range(A_KV_HEADS):
            kx, vx1 = _kv_operands(_spread_heads(kk, j), _spread_heads(vv, j))
            for p in (2 * j, 2 * j + 1):
                sl = slice(p * LANES, (p + 1) * LANES)
                o = _attend(q_ref[:, sl], kx, vx1, sink_ref[2 * p], sink_ref[2 * p + 1], allowed)
                ab_ref[:, sl] = o.astype(BF16)
        gmlp(ws_ref, bs_ref)


def _mixer(sinks, q, k, v, cache_k, cache_v, u, gv, w_p, w_s, b_p, b_s):
    row = lambda i: (i, 0)
    prev = lambda i: (jnp.maximum(i - 1, 0), 0)
    item = lambda i: (jnp.maximum(i - N_MIX_PROMPT, 0), 0, 0)
    c2 = lambda i: (0, 0)
    c3 = lambda i: (0, 0, 0)
    cache_block = (ITEMS_PER_TILE, WINDOW, KV_WIDTH)
    return pl.pallas_call(
        _mixer_kernel,
        grid=(N_MIX,),
        in_specs=[
            pl.BlockSpec(memory_space=pltpu.SMEM),
            pl.BlockSpec((MIX_TM, A_WIDTH), row),
            pl.BlockSpec((MIX_TM, KV_WIDTH), prev),
            pl.BlockSpec((MIX_TM, KV_WIDTH), row),
            pl.BlockSpec((MIX_TM, KV_WIDTH), prev),
            pl.BlockSpec((MIX_TM, KV_WIDTH), row),
            pl.BlockSpec(cache_block, item),
            pl.BlockSpec(cache_block, item),
            pl.BlockSpec((MIX_TM, B_WIDTH), row),
            pl.BlockSpec((MIX_TM, B_WIDTH), row),
            pl.BlockSpec((B_GROUPS, MIX_TM, MIX_TM), c3),
            pl.BlockSpec((B_GROUPS, MIX_TM, MIX_TM), c3),
            pl.BlockSpec((MIX_TM, B_GROUPS), c2),
            pl.BlockSpec((MIX_TM, B_GROUPS), c2),
        ],
        out_specs=[
            pl.BlockSpec((MIX_TM, A_WIDTH + B_WIDTH), row),
            pl.BlockSpec(cache_block, item),
            pl.BlockSpec(cache_block, item),
        ],
        out_shape=[
            jax.ShapeDtypeStruct((N_TOK, A_WIDTH + B_WIDTH), BF16),
            jax.ShapeDtypeStruct((DEC_BATCH, WINDOW, KV_WIDTH), F32),
            jax.ShapeDtypeStruct((DEC_BATCH, WINDOW, KV_WIDTH), F32),
        ],
        compiler_params=_params(("arbitrary",)),
        name="mixer",
    )(sinks, q, k, k, v, v, cache_k, cache_v, u, gv, w_p, w_s, b_p, b_s)


N_PJ = D_MODEL // PJ_BLOCK
N_FF = D_FF // FF_BLOCK


def _proj_ffn_kernel(*refs, split_in, final):
    refs = list(refs)
    h_refs = [refs.pop(0) for _ in range(2 if split_in else 1)]
    y_ref, wo_ref, gf_ref, wu_ref, wd_ref, gn_ref, o_ref = refs[:7]
    if final:
        ys_ref, xn_ref = refs[7:]
    else:
        (xn_ref,) = refs[7:]
    i = pl.program_id(0)
    s = pl.program_id(1)
    last_tile = i == N_TILES - 1
    cols = [slice(c * MXU_N, (c + 1) * MXU_N) for c in range(D_MODEL // MXU_N)]

    def accumulate(lhs, w_ref, first=False):
        for sl in cols:
            if first:
                o_ref[:, sl] = _dot(lhs, w_ref[:, sl].astype(BF16))
            else:
                o_ref[:, sl] += _dot(lhs, w_ref[:, sl].astype(BF16))

    @pl.when(s == 0)
    def _():
        accumulate(y_ref[...], wo_ref, first=True)

    @pl.when((s > 0) & (s < N_PJ))
    def _():
        accumulate(y_ref[...], wo_ref)

    for c in range(N_PJ):
        sl = slice(c * PJ_BLOCK, (c + 1) * PJ_BLOCK)
        if split_in:
            hp_ref, hs_ref = h_refs

            @pl.when((s == c) & jnp.logical_not(last_tile))
            def _():
                o_ref[:, sl] += hp_ref[...]

            @pl.when((s == c) & last_tile)
            def _():
                o_ref[0:TAIL, sl] += hp_ref[0:TAIL, :]
                o_ref[TAIL:, sl] += hs_ref[...]
        else:
            @pl.when(s == c)
            def _():
                o_ref[:, sl] += h_refs[0][...]

    @pl.when(s == N_PJ)
    def _():
        for a, b in _row_spans(0, TMB):
            xn_ref[a:b, :] = _rms(o_ref[a:b, :], gf_ref[...]).astype(BF16)

    @pl.when(s >= N_PJ)
    def _():
        t = jnp.maximum(_dot(xn_ref[...], wu_ref[...].astype(BF16)), 0.0)
        accumulate((t * t).astype(BF16), wd_ref)

    last_step = s == N_PJ + N_FF - 1
    if final:
        @pl.when(last_step & jnp.logical_not(last_tile))
        def _():
            for a, b in _row_spans(0, TMB):
                o_ref[a:b, :] = _rms(o_ref[a:b, :], gn_ref[...])

        @pl.when(last_step & last_tile)
        def _():
            for a, b in _row_spans(0, TAIL):
                o_ref[a:b, :] = _rms(o_ref[a:b, :], gn_ref[...])
            for a, b in _row_spans(0, N_SAMPLE):
                ys_ref[a:b, :] = _rms(o_ref[TAIL + a:TAIL + b, :], gn_ref[...])
    else:
        @pl.when(last_step)
        def _():
            for a, b in _row_spans(0, TMB):
                xn_ref[a:b, :] = _rms(o_ref[a:b, :], gn_ref[...]).astype(BF16)


def _proj_ffn(h, y, w_out, g_ffn, w_up, w_down, layer, g_next, final):
    split_in = isinstance(h, tuple)
    row = lambda i, s: (i, 0)
    const = lambda i, s: (0, 0)
    pj = lambda s: jnp.minimum(s, N_PJ - 1)
    ff = lambda s: jnp.maximum(s - N_PJ, 0)
    h_specs = [pl.BlockSpec((TMB, PJ_BLOCK), lambda i, s: (i, pj(s)))]
    if split_in:
        h_specs.append(pl.BlockSpec((N_SAMPLE, PJ_BLOCK), lambda i, s: (0, pj(s))))
    if final:
        out_specs = [pl.BlockSpec((TMB, D_MODEL), row), pl.BlockSpec((N_SAMPLE, D_MODEL), const)]
        out_shape = [jax.ShapeDtypeStruct((N_PROMPT, D_MODEL), F32),
                     jax.ShapeDtypeStruct((N_SAMPLE, D_MODEL), F32)]
        scratch = [pltpu.VMEM((TMB, D_MODEL), BF16)]
    else:
        out_specs = [pl.BlockSpec((TMB, D_MODEL), row), pl.BlockSpec((TMB, D_MODEL), row)]
        out_shape = [jax.ShapeDtypeStruct((N_TOK, D_MODEL), F32),
                     jax.ShapeDtypeStruct((N_TOK, D_MODEL), BF16)]
        scratch = []
    return pl.pallas_call(
        functools.partial(_proj_ffn_kernel, split_in=split_in, final=final),
        grid=(N_TILES, N_PJ + N_FF),
        in_specs=[
            *h_specs,
            pl.BlockSpec((TMB, PJ_BLOCK), lambda i, s: (i, pj(s))),
            pl.BlockSpec((PJ_BLOCK, D_MODEL), lambda i, s: (pj(s), 0)),
            pl.BlockSpec((1, D_MODEL), const),
            pl.BlockSpec((None, D_MODEL, FF_BLOCK), lambda i, s: (layer, 0, ff(s))),
            pl.BlockSpec((None, FF_BLOCK, D_MODEL), lambda i, s: (layer, ff(s), 0)),
            pl.BlockSpec((1, D_MODEL), const),
        ],
        out_specs=out_specs,
        out_shape=out_shape,
        scratch_shapes=scratch,
        compiler_params=_params(("arbitrary", "arbitrary")),
        name="proj_ffn_final" if final else "proj_ffn",
    )(*(h if split_in else (h,)), y, w_out, g_ffn, w_up, w_down, g_next)


N_HG_PROMPT = N_PROMPT // TM
N_HG = N_HG_PROMPT + 1
CHUNKS_PER_TILE = TM // HG_CHUNK
SAMPLE_CHUNKS = N_SAMPLE // HG_CHUNK


def _hgrn_scores(q, k, f, b_ref, r0, seg):
    n = HG_CHUNK
    sub = 8
    apart = [(lax.broadcasted_iota(jnp.int32, (sub, n), 0) + r * sub)
             ^ lax.broadcasted_iota(jnp.int32, (sub, n), 1) for r in range(n // sub)]
    scores = [None] * (n // sub)

    def merge(r, h, block):
        if scores[r] is None:
            scores[r] = block if 2 * h == n else jnp.where(apart[r] < 2 * h, block, 0.0)
        else:
            scores[r] = jnp.where(apart[r] < 2 * h, block, scores[r])

    def coarse(h):
        qs, ks = [], []
        zero = jnp.zeros((h, C_DIM), F32)
        for g in range(n // (2 * h)):
            base = r0 + g * 2 * h
            beta = b_ref[base + h - 1:base + h, :]
            lo = slice(g * 2 * h, g * 2 * h + h)
            up = slice(g * 2 * h + h, (g + 1) * 2 * h)
            ks += [k[lo] * jnp.exp2(beta - b_ref[base:base + h, :]), zero]
            qs.append(q[up] * jnp.exp2(b_ref[base + h:base + 2 * h, :] - beta))
        return jnp.concatenate(qs, axis=0).astype(BF16), jnp.concatenate(ks, axis=0).astype(BF16)

    def fine(h):
        if h == 4:
            qs, ks = [], []
            upper = lax.broadcasted_iota(jnp.int32, (8, C_DIM), 0) >= 4
            for g in range(n // 8):
                sl = slice(8 * g, 8 * g + 8)
                base = r0 + 8 * g
                e = jnp.exp2(-jnp.abs(b_ref[base:base + 8, :] - b_ref[base + 3:base + 4, :]))
                qs.append(jnp.where(upper, q[sl] * e, 0.0))
                ks.append(jnp.where(upper, 0.0, k[sl] * e))
            qt = jnp.concatenate(qs, axis=0)
            kt = jnp.concatenate(ks, axis=0)
        elif h == 2:
            r4 = lax.broadcasted_iota(jnp.int32, (n, C_DIM), 0) & 3
            f_next = pltpu.roll(f, n - 1, 0)
            f_prev = pltpu.roll(f, 1, 0)
            kt = jnp.where(r4 == 0, k * f_next, jnp.where(r4 == 1, k, 0.0))
            qt = jnp.where(r4 == 2, q * f, jnp.where(r4 == 3, q * f * f_prev, 0.0))
        else:
            odd = (lax.broadcasted_iota(jnp.int32, (n, C_DIM), 0) & 1) == 1
            kt = jnp.where(odd, 0.0, k)
            qt = jnp.where(odd, q * f, 0.0)
        return qt.astype(BF16), kt.astype(BF16)

    h = seg // 2
    while h >= 1:
        qt, kt = coarse(h) if h >= 8 else fine(h)
        s_h = lax.dot_general(qt, kt, _NT, preferred_element_type=F32)
        if h >= 8:
            per_group = h // sub
            for i in range(n // 2 // sub):
                g, j = divmod(i, per_group)
                merge((g * 2 * h + h) // sub + j, h, s_h[i * sub:(i + 1) * sub])
        else:
            for r in range(n // sub):
                merge(r, h, s_h[r * sub:(r + 1) * sub])
        h //= 2
    diag = jnp.sum(q * k, axis=1, keepdims=True)
    blocks = []
    for r in range(n // sub):
        lower = jnp.zeros((sub, n), F32) if scores[r] is None else scores[r]
        blocks.append(jnp.where(apart[r] == 0, diag[r * sub:(r + 1) * sub], lower))
    return jnp.concatenate(blocks, axis=0)


def _hgrn_kernel(xn_ref, xn_s_ref, wq_ref, wf_ref, wi_ref, wg_ref, lb_ref, go_ref, trip_ref,
                 tris_ref, s0_ref, og_ref, sp_ref, ss_ref, w_ref, z_ref, f_ref, b_ref, st_ref):
    t = pl.program_id(1)

    lbx = lb_ref[...]
    e = jnp.exp(lbx - jnp.max(lbx, axis=0, keepdims=True))
    p = e / jnp.sum(e, axis=0, keepdims=True)
    lb = (p[0:1, :] + p[1:2, :]) - p[0:1, :]

    @pl.when(t == 0)
    def _():
        st_ref[...] = jnp.zeros_like(st_ref)
        for c, src in enumerate((wq_ref, wf_ref, wi_ref, wg_ref)):
            w_ref[:, c * C_DIM:(c + 1) * C_DIM] = src[...].astype(BF16)

    def cumulative_log_decay(tri_ref, n_chunks):
        lf = jnp.log2(f_ref[0:n_chunks * HG_CHUNK, :])
        lf = jnp.concatenate([lf[c * HG_CHUNK:(c + 1) * HG_CHUNK] for c in range(n_chunks)], axis=1)
        hi = lf.astype(BF16)
        lo = (lf - hi.astype(F32)).astype(BF16)
        bb = _dot(tri_ref[...], jnp.concatenate([hi, lo], axis=1))
        for c in range(n_chunks):
            lanes = slice(c * C_DIM, (c + 1) * C_DIM)
            b_ref[c * HG_CHUNK:(c + 1) * HG_CHUNK, :] = bb[:, lanes] + bb[:, n_chunks * C_DIM:][:, lanes]

    def chunk(c, seg):
        r0 = c * HG_CHUNK
        rows = slice(r0, r0 + HG_CHUNK)
        zq = z_ref[rows, 0:C_DIM]
        f = f_ref[rows, :]
        v = z_ref[rows, 2 * C_DIM:3 * C_DIM]
        gate = _sigmoid(z_ref[rows, 3 * C_DIM:4 * C_DIM])
        q = zq * _sigmoid(zq)
        k = 1.0 - f
        scores = _hgrn_scores(q, k, f, b_ref, r0, seg)
        o = _dot(scores.astype(BF16), v.astype(BF16))
        b = b_ref[rows, :]
        qe = (q * jnp.exp2(b)).astype(BF16)
        if seg == HG_CHUNK:
            s_t = st_ref[...]
            o = o + lax.dot_general(qe, s_t.astype(BF16), _NT, preferred_element_type=F32)
            b_last = b_ref[r0 + HG_CHUNK - 1:r0 + HG_CHUNK, :]
            ke = (k * jnp.exp2(b_last - b)).astype(BF16)
            st_ref[...] = s_t * jnp.exp2(b_last) + _dot(v.T.astype(BF16), ke)
        else:
            outs = []
            v_t = v.T.astype(BF16)
            row = lax.broadcasted_iota(jnp.int32, (HG_CHUNK, C_DIM), 0)
            for a in range(HG_CHUNK // seg):
                item = c * (HG_CHUNK // seg) + a
                r = slice(a * seg, (a + 1) * seg)
                s_t = s0_ref[item].T
                outs.append(lax.dot_general(qe[r], s_t.astype(BF16), _NT, preferred_element_type=F32))
                b_last = b_ref[r0 + (a + 1) * seg - 1:r0 + (a + 1) * seg, :]
                own = (row >= a * seg) & (row < (a + 1) * seg)
                ke = jnp.where(own, k * jnp.exp2(jnp.minimum(b_last - b, 0.0)), 0.0).astype(BF16)
                s_new = s_t * jnp.exp2(b_last) + _dot(v_t, ke)
                ss_ref[item] = s_new.T
            o = o + jnp.concatenate(outs, axis=0)
        y = _rms(o, go_ref[...]) * gate
        og_ref[rows, :] = y.astype(BF16)

    def tile(src_ref, tri_ref, seg, n_chunks):
        rows = slice(0, n_chunks * HG_CHUNK)
        z_ref[rows, :] = _dot(src_ref[...], w_ref[...])
        f_ref[rows, :] = lb + (1.0 - lb) * _sigmoid(z_ref[rows, C_DIM:2 * C_DIM])
        cumulative_log_decay(tri_ref, n_chunks)
        for c in range(n_chunks):
            chunk(c, seg)

    @pl.when(t < N_HG_PROMPT)
    def _():
        tile(xn_ref, trip_ref, HG_CHUNK, CHUNKS_PER_TILE)

    @pl.when(t == N_HG_PROMPT - 1)
    def _():
        sp_ref[...] = st_ref[...].T

    @pl.when(t == N_HG_PROMPT)
    def _():
        tile(xn_s_ref, tris_ref, DEC_SEQ, SAMPLE_CHUNKS)


def _hgrn(xn, w_in, lb, g_onorm, tri_p, tri_s, state0):
    c2 = lambda j, t: (0, 0)
    head_col = lambda j, t: (0, j)
    w_specs = [pl.BlockSpec((D_MODEL, C_DIM), functools.partial(lambda j, t, c: (0, c * C_HEADS + j), c=c))
               for c in range(4)]
    return pl.pallas_call(
        _hgrn_kernel,
        grid=(C_HEADS, N_HG),
        in_specs=[
            pl.BlockSpec((TM, D_MODEL), lambda j, t: (jnp.minimum(t, N_HG_PROMPT - 1), 0)),
            pl.BlockSpec((N_SAMPLE, D_MODEL), lambda j, t: (N_PROMPT // N_SAMPLE, 0)),
            *w_specs,
            pl.BlockSpec((2, C_DIM), head_col),
            pl.BlockSpec((1, C_DIM), head_col),
            pl.BlockSpec((HG_CHUNK, HG_CHUNK), c2),
            pl.BlockSpec((HG_CHUNK, HG_CHUNK), c2),
            pl.BlockSpec((DEC_BATCH, None, C_DIM, C_DIM), lambda j, t: (0, j, 0, 0)),
        ],
        out_specs=[
            pl.BlockSpec((TM, C_DIM), lambda j, t: (t, j)),
            pl.BlockSpec((None, C_DIM, C_DIM), lambda j, t: (j, 0, 0)),
            pl.BlockSpec((DEC_BATCH, None, C_DIM, C_DIM), lambda j, t: (0, j, 0, 0)),
        ],
        out_shape=[
            jax.ShapeDtypeStruct((N_TOK, C_HEADS * C_DIM), BF16),
            jax.ShapeDtypeStruct((C_HEADS, C_DIM, C_DIM), F32),
            jax.ShapeDtypeStruct((DEC_BATCH, C_HEADS, C_DIM, C_DIM), F32),
        ],
        scratch_shapes=[
            pltpu.VMEM((D_MODEL, 4 * C_DIM), BF16),
            pltpu.VMEM((TM, 4 * C_DIM), F32),
            pltpu.VMEM((TM, C_DIM), F32),
            pltpu.VMEM((TM, C_DIM), F32),
            pltpu.VMEM((C_DIM, C_DIM), F32),
        ],
        compiler_params=_params(("arbitrary", "arbitrary")),
        name="hgrn",
    )(xn, xn, w_in, w_in, w_in, w_in, lb, g_onorm, tri_p, tri_s, state0)


def _rope_tables():
    half = ROT_DIM // 2
    inv = ROPE_THETA ** (-(jnp.arange(half, dtype=F32) * (2.0 / ROT_DIM)))
    pos = jnp.concatenate([jnp.arange(N_PROMPT, dtype=jnp.int32),
                           jnp.tile(PAST_LEN + jnp.arange(DEC_SEQ, dtype=jnp.int32), DEC_BATCH)])
    ang = pos.astype(F32)[:, None] * inv[None, :]
    cos, sin = jnp.cos(ang), jnp.sin(ang)
    ones = jnp.ones((N_TOK, HEAD_DIM - ROT_DIM), F32)
    ca = jnp.concatenate([cos, cos, ones], axis=1)
    cb = jnp.concatenate([-sin, sin, 0.0 * ones], axis=1)
    return jnp.tile(ca, (1, LANES // HEAD_DIM)), jnp.tile(cb, (1, LANES // HEAD_DIM))


def kernel(x_prompt, x_sample, cache_swa_k, cache_swa_v, state_hgrn, norm_mix, norm_ffn, norm_final,
           w_in_even, w_out_even, attn_sinks, gmlp_vnorm, gmlp_ws, gmlp_bias, w_in_odd, w_out_odd,
           hgrn_lb, hgrn_onorm, w_ffn_up, w_ffn_down):
    xp = x_prompt.reshape(N_PROMPT, D_MODEL)
    xs = x_sample.reshape(N_SAMPLE, D_MODEL)
    ca, cb = _rope_tables()

    q, k, v, u, gv = _even_in(xp, xs, norm_mix[0:1], w_in_even[0], ca, cb, gmlp_vnorm[0:1])
    ws = gmlp_ws[0]
    cidx = jnp.arange(MIX_TM) // CHUNK
    w_p = jnp.where(cidx[:, None] >= cidx[None, :], ws, 0.0).astype(BF16)
    eye = jnp.eye(ITEMS_PER_TILE, dtype=F32)
    w_s = jnp.einsum('ab,gpq->gapbq', eye, ws[:, :DEC_SEQ, :DEC_SEQ]).reshape(B_GROUPS, MIX_TM, MIX_TM).astype(BF16)
    b_p = gmlp_bias[0].T
    b_s = jnp.tile(gmlp_bias[0][:, :DEC_SEQ], (1, ITEMS_PER_TILE)).T
    ab, k_s, v_s = _mixer(attn_sinks[0], q, k, v,
                          cache_swa_k[0].reshape(DEC_BATCH, WINDOW, KV_WIDTH),
                          cache_swa_v[0].reshape(DEC_BATCH, WINDOW, KV_WIDTH),
                          u, gv, w_p, w_s, b_p, b_s)
    h1, xn1 = _proj_ffn((xp, xs), ab, w_out_even[0], norm_ffn[0:1], w_ffn_up, w_ffn_down, 0,
                        norm_mix[1:2], final=False)

    tri_p = jnp.tril(jnp.ones((HG_CHUNK, HG_CHUNK), F32))
    seg = jnp.arange(HG_CHUNK) // DEC_SEQ
    tri_s = jnp.where(seg[:, None] == seg[None, :], tri_p, 0.0)
    og, s_p, s_s = _hgrn(xn1, w_in_odd[0], hgrn_lb, hgrn_onorm[0:1], tri_p.astype(BF16), tri_s.astype(BF16),
                         state_hgrn[0])
    y_p, y_s = _proj_ffn(h1, og, w_out_odd[0], norm_ffn[1:2], w_ffn_up, w_ffn_down, 1,
                         norm_final[None, :], final=True)

    kv_shape = (1, 1, WINDOW, A_KV_HEADS, HEAD_DIM)
    kv_s_shape = (1, DEC_BATCH, WINDOW, A_KV_HEADS, HEAD_DIM)
    return (y_p.reshape(1, N_PROMPT, D_MODEL),
            y_s.reshape(DEC_BATCH, DEC_SEQ, D_MODEL),
            k[N_PROMPT - WINDOW:N_PROMPT].reshape(kv_shape),
            v[N_PROMPT - WINDOW:N_PROMPT].reshape(kv_shape),
            k_s.reshape(kv_s_shape),
            v_s.reshape(kv_s_shape),
            gv[N_PROMPT:].reshape(1, DEC_BATCH, DEC_SEQ, B_GROUPS, GMLP_DIM),
            s_p[None, None],
            s_s[None])
```
